```python
import math
import jax, jax.numpy as jnp
from jax import lax
import numpy as np

D_MODEL = 4096
BATCH = 4
SEQ = 4096
DEPTH = 1

N_META = 16
D_FF = 11008
D_RGLRU = D_MODEL // 2
RG_HEADS = 8
RG_HEAD_DIM = D_RGLRU // RG_HEADS
CONV_WIDTH = 4
RG_C = 8.0
D_S5 = D_MODEL - D_RGLRU
S5_GROUP = 16
S5_GROUPS = D_S5 // S5_GROUP
S5_STATE = 64
DT_MIN = 0.001
DT_MAX = 0.1
D_MIX = D_RGLRU + D_S5
D_IN_PROJ = 2 * D_RGLRU + D_S5
EPS = 1e-6

kernel_name = "hymba_rglru_s5_macaron_layer"


def rmsnorm(x, g):
    xf = x.astype(jnp.float32)
    y = xf * lax.rsqrt(jnp.mean(xf * xf, axis=-1, keepdims=True) + EPS)
    return (y * g.astype(jnp.float32)).astype(x.dtype)


def swiglu(h, w_gate, w_up, w_down):
    return (jax.nn.silu(h @ w_gate) * (h @ w_up)) @ w_down


def rg_lru_mixer(u, gate, conv_w, conv_b, w_a, b_a, w_x, b_x, lam):
    bsz, t_len, _ = u.shape
    up = jnp.pad(u, ((0, 0), (CONV_WIDTH - 1, 0), (0, 0)))
    xc = conv_b + sum(up[:, k:k + t_len] * conv_w[k] for k in range(CONV_WIDTH))
    xh = xc.reshape(bsz, t_len, RG_HEADS, RG_HEAD_DIM)
    r = jax.nn.sigmoid(jnp.einsum('bthi,hij->bthj', xh, w_a) + b_a).reshape(bsz, t_len, D_RGLRU)
    i = jax.nn.sigmoid(jnp.einsum('bthi,hij->bthj', xh, w_x) + b_x).reshape(bsz, t_len, D_RGLRU)
    log_a = -RG_C * r.astype(jnp.float32) * jax.nn.softplus(-lam.astype(jnp.float32))
    a = jnp.exp(log_a)
    mult = jnp.sqrt(-jnp.expm1(2.0 * log_a))
    bx = mult * i.astype(jnp.float32) * xc.astype(jnp.float32)

    def step(h, ab):
        a_t, b_t = ab
        h = a_t * h + b_t
        return h, h

    _, hs = lax.scan(step, jnp.zeros((bsz, D_RGLRU), jnp.float32),
                     (jnp.swapaxes(a, 0, 1), jnp.swapaxes(bx, 0, 1)))
    h = jnp.swapaxes(hs, 0, 1)
    return (h * jax.nn.gelu(gate.astype(jnp.float32))).astype(u.dtype)


def s5_mixer(u, lam_re, lam_im, log_dt, b_re, b_im, c_re, c_im, d, glu_w, glu_b):
    bsz, t_len, _ = u.shape
    f32 = jnp.float32
    dt = jnp.exp(log_dt.astype(f32))[:, None]
    lam = lax.complex(lam_re.astype(f32), lam_im.astype(f32))
    lam_bar = jnp.exp(lam * dt)
    b = lax.complex(b_re.astype(f32), b_im.astype(f32))
    b_bar = ((lam_bar - 1.0) / lam)[..., None] * b
    ug = u.astype(f32).reshape(bsz, t_len, S5_GROUPS, S5_GROUP)
    bu = jnp.einsum('btgc,gnc->btgn', ug, b_bar)
    a_elems = jnp.broadcast_to(lam_bar[None, None], (1, t_len, S5_GROUPS, S5_STATE))

    def combine(e_i, e_j):
        a_i, b_i = e_i
        a_j, b_j = e_j
        return (a_j * a_i, a_j * b_i + b_j)

    _, states = lax.associative_scan(combine, (a_elems, bu), axis=1)
    c = lax.complex(c_re.astype(f32), c_im.astype(f32))
    y = jnp.einsum('btgn,gcn->btgc', states, c).real
    y = y + d.astype(f32).reshape(S5_GROUPS, S5_GROUP) * ug
    y = y.reshape(bsz, t_len, D_S5).astype(u.dtype)
    z = jax.nn.gelu(y)
    return z * jax.nn.sigmoid(z @ glu_w + glu_b)


def setup_inputs(seed: int = 0) -> dict:
    key = jax.random.key(seed)
    ks = jax.random.split(key, 40)
    f32 = jnp.float32
    nrm = lambda k, shape, s: jax.random.normal(k, shape, f32) * s
    L = DEPTH
    u = jax.random.uniform(ks[30], (L, D_RGLRU), f32, 0.9, 0.999)
    rg_lambda = jnp.log(u ** (1.0 / RG_C)) - jnp.log1p(-(u ** (1.0 / RG_C)))
    n_idx = jnp.arange(S5_STATE, dtype=f32)
    s5_lambda_re = -0.5 + nrm(ks[31], (L, S5_GROUPS, S5_STATE), 0.01)
    s5_lambda_im = math.pi * n_idx + nrm(ks[32], (L, S5_GROUPS, S5_STATE), 0.01)
    s5_log_dt = jax.random.uniform(ks[33], (L, S5_GROUPS), f32, math.log(DT_MIN), math.log(DT_MAX))
    return {
        "x": nrm(ks[0], (BATCH, SEQ, D_MODEL), 1.0),
        "meta_tokens": nrm(ks[1], (N_META, D_MODEL), 1.0),
        "ffn1_norm": 1.0 + nrm(ks[2], (L, D_MODEL), 0.02),
        "ffn1_w_gate": nrm(ks[3], (L, D_MODEL, D_FF), D_MODEL ** -0.5),
        "ffn1_w_up": nrm(ks[4], (L, D_MODEL, D_FF), D_MODEL ** -0.5),
        "ffn1_w_down": nrm(ks[5], (L, D_FF, D_MODEL), D_FF ** -0.5),
        "mix_norm": 1.0 + nrm(ks[6], (L, D_MODEL), 0.02),
        "w_in": nrm(ks[7], (L, D_MODEL, D_IN_PROJ), D_MODEL ** -0.5),
        "rg_conv_w": nrm(ks[8], (L, CONV_WIDTH, D_RGLRU), CONV_WIDTH ** -0.5),
        "rg_conv_b": nrm(ks[9], (L, D_RGLRU), 0.01),
        "rg_w_a": nrm(ks[10], (L, RG_HEADS, RG_HEAD_DIM, RG_HEAD_DIM), RG_HEAD_DIM ** -0.5),
        "rg_b_a": nrm(ks[11], (L, RG_HEADS, RG_HEAD_DIM), 0.01),
        "rg_w_x": nrm(ks[12], (L, RG_HEADS, RG_HEAD_DIM, RG_HEAD_DIM), RG_HEAD_DIM ** -0.5),
        "rg_b_x": nrm(ks[13], (L, RG_HEADS, RG_HEAD_DIM), 0.01),
        "rg_lambda": rg_lambda,
        "s5_lambda_re": s5_lambda_re,
        "s5_lambda_im": s5_lambda_im,
        "s5_log_dt": s5_log_dt,
        "s5_b_re": nrm(ks[14], (L, S5_GROUPS, S5_STATE, S5_GROUP), (2.0 * S5_GROUP) ** -0.5),
        "s5_b_im": nrm(ks[15], (L, S5_GROUPS, S5_STATE, S5_GROUP), (2.0 * S5_GROUP) ** -0.5),
        "s5_c_re": nrm(ks[16], (L, S5_GROUPS, S5_GROUP, S5_STATE), 1.0),
        "s5_c_im": nrm(ks[17], (L, S5_GROUPS, S5_GROUP, S5_STATE), 1.0),
        "s5_d": nrm(ks[18], (L, D_S5), 0.5),
        "s5_glu_w": nrm(ks[19], (L, D_S5, D_S5), D_S5 ** -0.5),
        "s5_glu_b": nrm(ks[20], (L, D_S5), 0.01),
        "rg_out_norm": 1.0 + nrm(ks[21], (L, D_RGLRU), 0.02),
        "s5_out_norm": 1.0 + nrm(ks[22], (L, D_S5), 0.02),
        "w_out": nrm(ks[23], (L, D_MIX, D_MODEL), D_MIX ** -0.5),
        "ffn2_norm": 1.0 + nrm(ks[24], (L, D_MODEL), 0.02),
        "ffn2_w_gate": nrm(ks[25], (L, D_MODEL, D_FF), D_MODEL ** -0.5),
        "ffn2_w_up": nrm(ks[26], (L, D_MODEL, D_FF), D_MODEL ** -0.5),
        "ffn2_w_down": nrm(ks[27], (L, D_FF, D_MODEL), D_FF ** -0.5),
        "final_norm": 1.0 + nrm(ks[28], (D_MODEL,), 0.02),
    }


def reference(x, meta_tokens, ffn1_norm, ffn1_w_gate, ffn1_w_up, ffn1_w_down, mix_norm, w_in,
              rg_conv_w, rg_conv_b, rg_w_a, rg_b_a, rg_w_x, rg_b_x, rg_lambda,
              s5_lambda_re, s5_lambda_im, s5_log_dt, s5_b_re, s5_b_im, s5_c_re, s5_c_im, s5_d,
              s5_glu_w, s5_glu_b, rg_out_norm, s5_out_norm, w_out,
              ffn2_norm, ffn2_w_gate, ffn2_w_up, ffn2_w_down, final_norm):
    bsz = x.shape[0]
    meta = jnp.broadcast_to(meta_tokens[None].astype(x.dtype), (bsz, N_META, D_MODEL))
    h = jnp.concatenate([meta, x], axis=1)
    for l in range(DEPTH):
        h = h + 0.5 * swiglu(rmsnorm(h, ffn1_norm[l]), ffn1_w_gate[l], ffn1_w_up[l], ffn1_w_down[l])
        proj = rmsnorm(h, mix_norm[l]) @ w_in[l]
        u_rg, g_rg, u_s5 = jnp.split(proj, [D_RGLRU, 2 * D_RGLRU], axis=-1)
        y_rg = rg_lru_mixer(u_rg, g_rg, rg_conv_w[l], rg_conv_b[l], rg_w_a[l], rg_b_a[l],
                            rg_w_x[l], rg_b_x[l], rg_lambda[l])
        y_s5 = s5_mixer(u_s5, s5_lambda_re[l], s5_lambda_im[l], s5_log_dt[l], s5_b_re[l], s5_b_im[l],
                        s5_c_re[l], s5_c_im[l], s5_d[l], s5_glu_w[l], s5_glu_b[l])
        y = jnp.concatenate([rmsnorm(y_rg, rg_out_norm[l]), rmsnorm(y_s5, s5_out_norm[l])], axis=-1)
        h = h + y @ w_out[l]
        h = h + 0.5 * swiglu(rmsnorm(h, ffn2_norm[l]), ffn2_w_gate[l], ffn2_w_up[l], ffn2_w_down[l])
    out = rmsnorm(h, final_norm)
    return out[:, N_META:]
```

```python
import functools
import math

import jax
import jax.numpy as jnp
from jax import lax
from jax.experimental import pallas as pl
from jax.experimental.pallas import tpu as pltpu

EPS = 1e-6
RG_C = 8.0
F32 = jnp.float32
BF16 = jnp.bfloat16

V7X_LANES = 128
V7X_SUBLANES = 8
V7X_VMEM_BYTES = 64 * 1024 * 1024
V7X_VMEM_BUDGET = 56 * 1024 * 1024


def _divisor_tile(n, pref, align):
    if n <= pref:
        return n
    t = (pref // align) * align
    while t >= align:
        if n % t == 0:
            return t
        t -= align
    return n


def _params(sem, vmem_bytes):
    limit = int(min(V7X_VMEM_BYTES - (2 << 20), max(vmem_bytes + (6 << 20), 32 << 20)))
    return pltpu.CompilerParams(dimension_semantics=sem, vmem_limit_bytes=limit)


def _rms_rows(x, g):
    ms = jnp.mean(x * x, axis=-1, keepdims=True)
    return x * lax.rsqrt(ms + EPS) * g


def _sigmoid(x):
    return 1.0 / (1.0 + jnp.exp(-x))


def _gelu_tanh(x):
    c = math.sqrt(2.0 / math.pi)
    return 0.5 * x * (1.0 + jnp.tanh(c * (x + 0.044715 * (x * x * x))))


def _row_chunk(tm):
    return _divisor_tile(tm, 32, V7X_SUBLANES)


def _ffn_body(x_ref, g_ref, wgu_ref, wd_ref, g2_ref, o_ref, hn_ref, *, tf, n_split, final_norm):
    j = pl.program_id(1)
    tm, d = x_ref.shape
    rc = _row_chunk(tm)

    @pl.when(j == 0)
    def _():
        def body(c, _):
            r0 = pl.multiple_of(c * rc, rc)
            x = x_ref[pl.ds(r0, rc), :]
            hn_ref[pl.ds(r0, rc), :] = _rms_rows(x, g_ref[...]).astype(BF16)
            return 0

        lax.fori_loop(0, tm // rc, body, 0)
        o_ref[...] = jnp.zeros_like(o_ref)

    gu = jnp.dot(hn_ref[...], wgu_ref[...], preferred_element_type=F32)
    g = gu[:, :tf]
    u = gu[:, tf:]
    a = (g * _sigmoid(g) * u).astype(BF16)
    dn = d // n_split
    for s in range(n_split):
        o_ref[:, s * dn:(s + 1) * dn] += jnp.dot(a, wd_ref[:, s * dn:(s + 1) * dn],
                                                 preferred_element_type=F32)

    @pl.when(j == pl.num_programs(1) - 1)
    def _():
        def body(c, _):
            r0 = pl.multiple_of(c * rc, rc)
            h = x_ref[pl.ds(r0, rc), :] + 0.5 * o_ref[pl.ds(r0, rc), :]
            if final_norm:
                h = _rms_rows(h, g2_ref[...])
            o_ref[pl.ds(r0, rc), :] = h
            return 0

        lax.fori_loop(0, tm // rc, body, 0)


def _ffn(x, norm_w, wgu, wd, norm2_w, *, final_norm):
    m, d = x.shape
    nf, _, tf2 = wgu.shape
    tf = tf2 // 2
    tm = _divisor_tile(m, 512, 16)
    n_split = max(1, d // 1024)
    vmem = 2 * tm * d * 4 * 2 + tm * d * 2 + 2 * (d * tf2 * 2) + 2 * (tf * d * 2) + tm * tf2 * 4 * 2
    body = functools.partial(_ffn_body, tf=tf, n_split=n_split, final_norm=final_norm)
    return pl.pallas_call(
        body,
        grid=(m // tm, nf),
        in_specs=[
            pl.BlockSpec((tm, d), lambda i, j: (i, 0)),
            pl.BlockSpec((1, d), lambda i, j: (0, 0)),
            pl.BlockSpec((None, d, tf2), lambda i, j: (j, 0, 0)),
            pl.BlockSpec((tf, d), lambda i, j: (j, 0)),
            pl.BlockSpec((1, d), lambda i, j: (0, 0)),
        ],
        out_specs=pl.BlockSpec((tm, d), lambda i, j: (i, 0)),
        out_shape=jax.ShapeDtypeStruct((m, d), F32),
        scratch_shapes=[pltpu.VMEM((tm, d), BF16)],
        compiler_params=_params(("parallel", "arbitrary"), vmem),
        name="ffn_swiglu",
    )(x, norm_w.reshape(1, d), wgu, wd, norm2_w.reshape(1, d))


def _inproj_body(x_ref, g_ref, w_ref, o_ref, hn_ref):
    tm = x_ref.shape[0]
    rc = _row_chunk(tm)

    @pl.when(pl.program_id(1) == 0)
    def _():
        def body(c, _):
            r0 = pl.multiple_of(c * rc, rc)
            hn_ref[pl.ds(r0, rc), :] = _rms_rows(x_ref[pl.ds(r0, rc), :], g_ref[...]).astype(BF16)
            return 0

        lax.fori_loop(0, tm // rc, body, 0)

    o_ref[...] = jnp.dot(hn_ref[...], w_ref[...], preferred_element_type=F32)


def _inproj(x, norm_w, w):
    m, d = x.shape
    n = w.shape[1]
    tm = _divisor_tile(m, 512, 16)
    tn = _divisor_tile(n, 1024, V7X_LANES)
    vmem = 2 * tm * d * 4 + tm * d * 2 + 2 * d * tn * 2 + 2 * tm * tn * 4
    return pl.pallas_call(
        _inproj_body,
        grid=(m // tm, n // tn),
        in_specs=[
            pl.BlockSpec((tm, d), lambda i, j: (i, 0)),
            pl.BlockSpec((1, d), lambda i, j: (0, 0)),
            pl.BlockSpec((d, tn), lambda i, j: (0, j)),
        ],
        out_specs=pl.BlockSpec((tm, tn), lambda i, j: (i, j)),
        out_shape=jax.ShapeDtypeStruct((m, n), F32),
        scratch_shapes=[pltpu.VMEM((tm, d), BF16)],
        compiler_params=_params(("parallel", "arbitrary"), vmem),
        name="in_proj",
    )(x, norm_w.reshape(1, d), w)


def _group_scan_real(a, b):
    row = lax.broadcasted_iota(jnp.int32, a.shape, 1)
    for dist in (1, 2, 4):
        keep = row >= dist
        a_prev = jnp.where(keep, pltpu.roll(a, dist, axis=1), 1.0)
        b_prev = jnp.where(keep, pltpu.roll(b, dist, axis=1), 0.0)
        b = a * b_prev + b
        a = a * a_prev
    return a, b


def _rg_body(u_ref, gate_ref, cw_ref, cb_ref, wax_ref, bax_ref, sp_ref, h0_ref, tail0_ref,
             y_ref, ht_ref, tailt_ref, ext_ref, a_ref, b_ref, hc_ref, *, conv_width):
    t = pl.program_id(2)
    tc, hd = u_ref.shape
    ng = tc // V7X_SUBLANES

    @pl.when(t == 0)
    def _():
        ext_ref[0:8, :] = tail0_ref[...]
        hc_ref[...] = h0_ref[...]

    ext_ref[8:8 + tc, :] = u_ref[...]
    xc = cb_ref[...] + cw_ref[conv_width - 1:conv_width, :] * u_ref[...]
    for k in range(conv_width - 1):
        back = conv_width - 1 - k
        xc = xc + cw_ref[k:k + 1, :] * ext_ref[pl.ds(8 - back, tc), :]
    ext_ref[0:8, :] = ext_ref[tc:tc + 8, :]

    pre = jnp.dot(xc.astype(BF16), wax_ref[...], preferred_element_type=F32) + bax_ref[...]
    r = _sigmoid(pre[:, :hd])
    i = _sigmoid(pre[:, hd:])
    log_a = (-RG_C) * r * sp_ref[...]
    a = jnp.exp(log_a)
    b = jnp.sqrt(1.0 - jnp.exp(2.0 * log_a)) * i * xc
    a3, b3 = _group_scan_real(a.reshape(ng, 8, hd), b.reshape(ng, 8, hd))
    a_ref[...] = a3.reshape(tc, hd)
    b_ref[...] = b3.reshape(tc, hd)
    y_ref[...] = _gelu_tanh(gate_ref[...])

    def body(g, hc):
        r0 = pl.multiple_of(g * 8, 8)
        h = a_ref[pl.ds(r0, 8), :] * hc + b_ref[pl.ds(r0, 8), :]
        y_ref[pl.ds(r0, 8), :] = h * y_ref[pl.ds(r0, 8), :]
        return jnp.broadcast_to(h[7:8, :], (8, hd))

    hc = lax.fori_loop(0, ng, body, hc_ref[...])
    hc_ref[...] = hc

    @pl.when(t == pl.num_programs(2) - 1)
    def _():
        ht_ref[...] = hc
        tailt_ref[...] = ext_ref[0:8, :]


def _rg_lru(proj, conv_w, conv_b, wax, bax, softplus_neg_lam, h0, tail0, *, d_rg):
    bsz, t_len, _ = proj.shape
    heads, hd, _ = wax.shape
    conv_width = conv_w.shape[0]
    tc = _divisor_tile(t_len, 512, V7X_SUBLANES)
    vmem = 2 * 3 * tc * hd * 4 + (tc + 8) * hd * 4 + 2 * tc * hd * 4 + 2 * hd * 2 * hd * 2
    body = functools.partial(_rg_body, conv_width=conv_width)
    hsel = lambda b, h, t: (0, 0, h)
    return pl.pallas_call(
        body,
        grid=(bsz, heads, t_len // tc),
        in_specs=[
            pl.BlockSpec((None, tc, hd), lambda b, h, t: (b, t, h)),
            pl.BlockSpec((None, tc, hd), lambda b, h, t: (b, t, heads + h)),
            pl.BlockSpec((conv_width, hd), lambda b, h, t: (0, h)),
            pl.BlockSpec((1, hd), lambda b, h, t: (0, h)),
            pl.BlockSpec((None, hd, 2 * hd), lambda b, h, t: (h, 0, 0)),
            pl.BlockSpec((None, 1, 2 * hd), lambda b, h, t: (h, 0, 0)),
            pl.BlockSpec((1, hd), lambda b, h, t: (0, h)),
            pl.BlockSpec((None, 8, hd), hsel),
            pl.BlockSpec((None, 8, hd), hsel),
        ],
        out_specs=[
            pl.BlockSpec((None, tc, hd), lambda b, h, t: (b, t, h)),
            pl.BlockSpec((None, 8, hd), lambda b, h, t: (b, 0, h)),
            pl.BlockSpec((None, 8, hd), lambda b, h, t: (b, 0, h)),
        ],
        out_shape=[
            jax.ShapeDtypeStruct((bsz, t_len, d_rg), F32),
            jax.ShapeDtypeStruct((bsz, 8, d_rg), F32),
            jax.ShapeDtypeStruct((bsz, 8, d_rg), F32),
        ],
        scratch_shapes=[
            pltpu.VMEM((tc + 8, hd), F32),
            pltpu.VMEM((tc, hd), F32),
            pltpu.VMEM((tc, hd), F32),
            pltpu.VMEM((8, hd), F32),
        ],
        compiler_params=_params(("parallel", "parallel", "arbitrary"), vmem),
        name="rg_lru",
    )(proj, proj, conv_w, conv_b.reshape(1, d_rg), wax, bax, softplus_neg_lam.reshape(1, d_rg),
      h0, tail0)


def _s5_body(u_ref, bmat_ref, lam_ref, p8_ref, cmat_ref, d_ref, x0_ref, z_ref, xt_ref,
             xs_ref, xc_ref):
    t = pl.program_id(2)
    tc = u_ref.shape[0]
    ns = lam_ref.shape[1] // 2
    ng = tc // V7X_SUBLANES

    @pl.when(t == 0)
    def _():
        xc_ref[...] = x0_ref[...]

    u = u_ref[...]
    bu = jnp.dot(u.astype(BF16), bmat_ref[...], preferred_element_type=F32)
    xr = bu[:, :ns].reshape(ng, 8, ns)
    xi = bu[:, ns:].reshape(ng, 8, ns)
    row = lax.broadcasted_iota(jnp.int32, xr.shape, 1)
    for k, dist in enumerate((1, 2, 4)):
        lr = lam_ref[k:k + 1, :ns]
        li = lam_ref[k:k + 1, ns:]
        keep = row >= dist
        pr = jnp.where(keep, pltpu.roll(xr, dist, axis=1), 0.0)
        pi = jnp.where(keep, pltpu.roll(xi, dist, axis=1), 0.0)
        xr, xi = xr + (lr * pr - li * pi), xi + (lr * pi + li * pr)
    xs_ref[:, :ns] = xr.reshape(tc, ns)
    xs_ref[:, ns:] = xi.reshape(tc, ns)

    p8r = p8_ref[:, :ns]
    p8i = p8_ref[:, ns:]

    def body(g, carry):
        cr, ci = carry
        r0 = pl.multiple_of(g * 8, 8)
        nr = xs_ref[pl.ds(r0, 8), :ns] + (p8r * cr - p8i * ci)
        ni = xs_ref[pl.ds(r0, 8), ns:] + (p8r * ci + p8i * cr)
        xs_ref[pl.ds(r0, 8), :ns] = nr
        xs_ref[pl.ds(r0, 8), ns:] = ni
        return (jnp.broadcast_to(nr[7:8, :], (8, ns)), jnp.broadcast_to(ni[7:8, :], (8, ns)))

    cr, ci = lax.fori_loop(0, ng, body, (xc_ref[:, :ns], xc_ref[:, ns:]))
    xc_ref[:, :ns] = cr
    xc_ref[:, ns:] = ci

    y = jnp.dot(xs_ref[...].astype(BF16), cmat_ref[...], preferred_element_type=F32)
    y = y + d_ref[...] * u
    z_ref[...] = _gelu_tanh(y)

    @pl.when(t == pl.num_programs(2) - 1)
    def _():
        xt_ref[...] = xc_ref[...]


def _s5_scan(proj, bmat, lam_pow, p8, cmat, d_skip, x0, *, col0, d_s5):
    bsz, t_len, _ = proj.shape
    nslab, slab, ns2 = bmat.shape
    tc = _divisor_tile(t_len, 512, V7X_SUBLANES)
    cb0 = col0 // slab
    vmem = 2 * 2 * tc * slab * 4 + 2 * tc * ns2 * 4 + 4 * slab * ns2 * 2 + 3 * tc * ns2 * 4
    return pl.pallas_call(
        _s5_body,
        grid=(bsz, nslab, t_len // tc),
        in_specs=[
            pl.BlockSpec((None, tc, slab), lambda b, s, t: (b, t, cb0 + s)),
            pl.BlockSpec((None, slab, ns2), lambda b, s, t: (s, 0, 0)),
            pl.BlockSpec((None, 8, ns2), lambda b, s, t: (s, 0, 0)),
            pl.BlockSpec((None, 8, ns2), lambda b, s, t: (s, 0, 0)),
            pl.BlockSpec((None, ns2, slab), lambda b, s, t: (s, 0, 0)),
            pl.BlockSpec((1, slab), lambda b, s, t: (0, s)),
            pl.BlockSpec((None, None, 8, ns2), lambda b, s, t: (0, s, 0, 0)),
        ],
        out_specs=[
            pl.BlockSpec((None, tc, slab), lambda b, s, t: (b, t, s)),
            pl.BlockSpec((None, None, 8, ns2), lambda b, s, t: (b, s, 0, 0)),
        ],
        out_shape=[
            jax.ShapeDtypeStruct((bsz, t_len, d_s5), F32),
            jax.ShapeDtypeStruct((bsz, nslab, 8, ns2), F32),
        ],
        scratch_shapes=[pltpu.VMEM((tc, ns2), F32), pltpu.VMEM((8, ns2), F32)],
        compiler_params=_params(("parallel", "parallel", "arbitrary"), vmem),
        name="s5_scan",
    )(proj, bmat, lam_pow, p8, cmat, d_skip.reshape(1, d_s5), x0)


def _glu_body(z_ref, zc_ref, w_ref, b_ref, o_ref, zb_ref):
    tm = z_ref.shape[0]
    rc = _row_chunk(tm)

    @pl.when(pl.program_id(1) == 0)
    def _():
        def body(c, _):
            r0 = pl.multiple_of(c * rc, rc)
            zb_ref[pl.ds(r0, rc), :] = z_ref[pl.ds(r0, rc), :].astype(BF16)
            return 0

        lax.fori_loop(0, tm // rc, body, 0)

    pre = jnp.dot(zb_ref[...], w_ref[...], preferred_element_type=F32) + b_ref[...]
    o_ref[...] = zc_ref[...] * _sigmoid(pre)


def _glu(z, w, b):
    m, d = z.shape
    tm = _divisor_tile(m, 1024, 16)
    tn = _divisor_tile(d, 512, V7X_LANES)
    vmem = 2 * tm * d * 4 + tm * d * 2 + 2 * d * tn * 2 + 4 * tm * tn * 4
    return pl.pallas_call(
        _glu_body,
        grid=(m // tm, d // tn),
        in_specs=[
            pl.BlockSpec((tm, d), lambda i, j: (i, 0)),
            pl.BlockSpec((tm, tn), lambda i, j: (i, j)),
            pl.BlockSpec((d, tn), lambda i, j: (0, j)),
            pl.BlockSpec((1, tn), lambda i, j: (0, j)),
        ],
        out_specs=pl.BlockSpec((tm, tn), lambda i, j: (i, j)),
        out_shape=jax.ShapeDtypeStruct((m, d), F32),
        scratch_shapes=[pltpu.VMEM((tm, d), BF16)],
        compiler_params=_params(("parallel", "arbitrary"), vmem),
        name="s5_glu",
    )(z, z, w, b.reshape(1, d))


def _outproj_body(yr_ref, ys_ref, gr_ref, gs_ref, w_ref, h_ref, o_ref, yn_ref):
    tm, d_rg = yr_ref.shape
    rc = _row_chunk(tm)

    @pl.when(pl.program_id(1) == 0)
    def _():
        def body(c, _):
            r0 = pl.multiple_of(c * rc, rc)
            yn_ref[pl.ds(r0, rc), :d_rg] = _rms_rows(yr_ref[pl.ds(r0, rc), :], gr_ref[...]).astype(BF16)
            yn_ref[pl.ds(r0, rc), d_rg:] = _rms_rows(ys_ref[pl.ds(r0, rc), :], gs_ref[...]).astype(BF16)
            return 0

        lax.fori_loop(0, tm // rc, body, 0)

    o_ref[...] = h_ref[...] + jnp.dot(yn_ref[...], w_ref[...], preferred_element_type=F32)


def _outproj(y_rg, y_s5, g_rg, g_s5, w, h):
    m, d_rg = y_rg.shape
    d_s5 = y_s5.shape[1]
    d_mix, d = w.shape
    tm = _divisor_tile(m, 512, 16)
    tn = _divisor_tile(d, 1024, V7X_LANES)
    vmem = 2 * tm * d_mix * 4 + tm * d_mix * 2 + 2 * d_mix * tn * 2 + 4 * tm * tn * 4
    return pl.pallas_call(
        _outproj_body,
        grid=(m // tm, d // tn),
        in_specs=[
            pl.BlockSpec((tm, d_rg), lambda i, j: (i, 0)),
            pl.BlockSpec((tm, d_s5), lambda i, j: (i, 0)),
            pl.BlockSpec((1, d_rg), lambda i, j: (0, 0)),
            pl.BlockSpec((1, d_s5), lambda i, j: (0, 0)),
            pl.BlockSpec((d_mix, tn), lambda i, j: (0, j)),
            pl.BlockSpec((tm, tn), lambda i, j: (i, j)),
        ],
        out_specs=pl.BlockSpec((tm, tn), lambda i, j: (i, j)),
        out_shape=jax.ShapeDtypeStruct((m, d), F32),
        scratch_shapes=[pltpu.VMEM((tm, d_mix), BF16)],
        compiler_params=_params(("parallel", "arbitrary"), vmem),
        name="out_proj",
    )(y_rg, y_s5, g_rg.reshape(1, d_rg), g_s5.reshape(1, d_s5), w, h)


def _ffn_weights(w_gate, w_up, w_down):
    d, f = w_gate.shape
    tf = _divisor_tile(f, 256, V7X_LANES)
    nf = f // tf
    wg = w_gate.reshape(d, nf, tf)
    wu = w_up.reshape(d, nf, tf)
    wgu = jnp.concatenate([wg, wu], axis=-1).transpose(1, 0, 2).astype(BF16)
    return wgu, w_down.astype(BF16)


def _s5_tables(lam_re, lam_im, log_dt, b_re, b_im, c_re, c_im, slab):
    g, n = lam_re.shape
    c = b_re.shape[-1]
    gs = slab // c
    nslab = g // gs
    dt = jnp.exp(log_dt.astype(F32))[:, None]
    lam = lax.complex(lam_re.astype(F32), lam_im.astype(F32))
    lam_dt = lam * dt
    lam_bar = jnp.exp(lam_dt)
    b_bar = ((lam_bar - 1.0) / lam)[..., None] * lax.complex(b_re.astype(F32), b_im.astype(F32))

    def powers(ks):
        p = jnp.exp(lam_dt[None] * jnp.asarray(ks, F32)[:, None, None])
        p = p.reshape(len(ks), nslab, gs * n)
        return jnp.concatenate([jnp.real(p), jnp.imag(p)], axis=-1).transpose(1, 0, 2)

    lam_pow = powers([1, 2, 4, 8, 8, 8, 8, 8])
    p8 = powers([1, 2, 3, 4, 5, 6, 7, 8])

    eye = jnp.eye(gs, dtype=F32)
    bb = b_bar.reshape(nslab, gs, n, c)
    b_blk = jnp.einsum('sgnc,gh->sgchn', bb, eye.astype(bb.dtype))
    b_blk = b_blk.reshape(nslab, gs * c, gs * n)
    bmat = jnp.concatenate([jnp.real(b_blk), jnp.imag(b_blk)], axis=-1).astype(BF16)
    cr = c_re.astype(F32).reshape(nslab, gs, c, n)
    ci = c_im.astype(F32).reshape(nslab, gs, c, n)
    cr_blk = jnp.einsum('sgcn,gh->sgnhc', cr, eye).reshape(nslab, gs * n, gs * c)
    ci_blk = jnp.einsum('sgcn,gh->sgnhc', ci, eye).reshape(nslab, gs * n, gs * c)
    cmat = jnp.concatenate([cr_blk, -ci_blk], axis=1).astype(BF16)
    return bmat, lam_pow, p8, cmat


def kernel(x, meta_tokens, ffn1_norm, ffn1_w_gate, ffn1_w_up, ffn1_w_down, mix_norm, w_in,
           rg_conv_w, rg_conv_b, rg_w_a, rg_b_a, rg_w_x, rg_b_x, rg_lambda,
           s5_lambda_re, s5_lambda_im, s5_log_dt, s5_b_re, s5_b_im, s5_c_re, s5_c_im, s5_d,
           s5_glu_w, s5_glu_b, rg_out_norm, s5_out_norm, w_out,
           ffn2_norm, ffn2_w_gate, ffn2_w_up, ffn2_w_down, final_norm):
    bsz, t_len, d = x.shape
    n_meta = meta_tokens.shape[0]
    depth = ffn1_norm.shape[0]
    d_rg = rg_lambda.shape[-1]
    d_s5 = s5_d.shape[-1]
    heads, hd = rg_w_a.shape[1], rg_w_a.shape[2]
    slab = V7X_LANES

    h = x.reshape(bsz * t_len, d)
    hm = meta_tokens.astype(x.dtype)
    for l in range(depth):
        last = l == depth - 1
        wgu1, wd1 = _ffn_weights(ffn1_w_gate[l], ffn1_w_up[l], ffn1_w_down[l])
        wgu2, wd2 = _ffn_weights(ffn2_w_gate[l], ffn2_w_up[l], ffn2_w_down[l])
        w_in_b = w_in[l].astype(BF16)
        w_out_b = w_out[l].astype(BF16)
        glu_w_b = s5_glu_w[l].astype(BF16)
        wax = jnp.concatenate([rg_w_a[l], rg_w_x[l]], axis=-1).astype(BF16)
        bax = jnp.concatenate([rg_b_a[l], rg_b_x[l]], axis=-1).reshape(heads, 1, 2 * hd)
        sp = jax.nn.softplus(-rg_lambda[l].astype(F32))
        bmat, lam_pow, p8, cmat = _s5_tables(s5_lambda_re[l], s5_lambda_im[l], s5_log_dt[l], s5_b_re[l],
                                             s5_b_im[l], s5_c_re[l], s5_c_im[l], slab)
        nslab, _, ns2 = bmat.shape
        fnorm = final_norm if last else ffn2_norm[l]

        def mixers(hrows, nb, h0, tail0, x0):
            h1 = _ffn(hrows, ffn1_norm[l], wgu1, wd1, ffn1_norm[l], final_norm=False)
            proj = _inproj(h1, mix_norm[l], w_in_b).reshape(nb, hrows.shape[0] // nb, -1)
            y_rg, h_t, tail_t = _rg_lru(proj, rg_conv_w[l], rg_conv_b[l], wax, bax, sp, h0, tail0, d_rg=d_rg)
            z, x_t = _s5_scan(proj, bmat, lam_pow, p8, cmat, s5_d[l], x0, col0=2 * d_rg, d_s5=d_s5)
            return h1, y_rg, z, h_t, tail_t, x_t

        def tail_layers(h1, y_rg, z):
            m = h1.shape[0]
            y_s5 = _glu(z.reshape(m, d_s5), glu_w_b, s5_glu_b[l])
            h2 = _outproj(y_rg.reshape(m, d_rg), y_s5, rg_out_norm[l], s5_out_norm[l], w_out_b, h1)
            return _ffn(h2, ffn2_norm[l], wgu2, wd2, fnorm, final_norm=last)

        zeros_rg = jnp.zeros((1, 8, d_rg), F32)
        zeros_s5 = jnp.zeros((1, nslab, 8, ns2), F32)
        h1m, y_rgm, zm, h_t, tail_t, x_t = mixers(hm, 1, zeros_rg, zeros_rg, zeros_s5)
        h1, y_rg, z, _, _, _ = mixers(h, bsz, h_t, tail_t, x_t)
        h = tail_layers(h1, y_rg, z)
        if not last:
            hm = tail_layers(h1m, y_rgm, zm)
    return h.reshape(bsz, t_len, d)
```

```python
import functools
import math

import jax
import jax.numpy as jnp
from jax import lax
from jax.experimental import pallas as pl
from jax.experimental.pallas import tpu as pltpu

EPS = 1e-6
RG_C = 8.0
F32 = jnp.float32
BF16 = jnp.bfloat16

V7X_LANES = 128
V7X_SUBLANES = 8
V7X_VMEM_BYTES = 64 * 1024 * 1024

S5_CHUNK = 8


def _divisor_tile(n, pref, align):
    if n <= pref:
        return n
    t = (pref // align) * align
    while t >= align:
        if n % t == 0:
            return t
        t -= align
    return n


def _params(sem, vmem_bytes):
    limit = int(min(V7X_VMEM_BYTES - (2 << 20), max(vmem_bytes + (6 << 20), 32 << 20)))
    return pltpu.CompilerParams(dimension_semantics=sem, vmem_limit_bytes=limit)


def _rms_rows(x, g):
    ms = jnp.mean(x * x, axis=-1, keepdims=True)
    return x * lax.rsqrt(ms + EPS) * g


def _sigmoid(x):
    return 1.0 / (1.0 + jnp.exp(-x))


def _gelu_tanh(x):
    c = math.sqrt(2.0 / math.pi)
    return 0.5 * x * (1.0 + jnp.tanh(c * (x + 0.044715 * (x * x * x))))


def _row_chunk(tm):
    return _divisor_tile(tm, 32, V7X_SUBLANES)


def _ffn_body(x_ref, g_ref, wg_ref, wu_ref, wd_ref, g2_ref, o_ref, hn_ref, *, n_split, final_norm):
    j = pl.program_id(1)
    tm, d = x_ref.shape
    rc = _row_chunk(tm)

    @pl.when(j == 0)
    def _():
        def body(c, _):
            r0 = pl.multiple_of(c * rc, rc)
            x = x_ref[pl.ds(r0, rc), :]
            hn_ref[pl.ds(r0, rc), :] = _rms_rows(x, g_ref[...]).astype(BF16)
            return 0

        lax.fori_loop(0, tm // rc, body, 0)
        o_ref[...] = jnp.zeros_like(o_ref)

    g = jnp.dot(hn_ref[...], wg_ref[...], preferred_element_type=F32)
    u = jnp.dot(hn_ref[...], wu_ref[...], preferred_element_type=F32)
    a = (g * _sigmoid(g) * u).astype(BF16)
    dn = d // n_split
    for s in range(n_split):
        o_ref[:, s * dn:(s + 1) * dn] += jnp.dot(a, wd_ref[:, s * dn:(s + 1) * dn],
                                                 preferred_element_type=F32)

    @pl.when(j == pl.num_programs(1) - 1)
    def _():
        def body(c, _):
            r0 = pl.multiple_of(c * rc, rc)
            h = x_ref[pl.ds(r0, rc), :] + 0.5 * o_ref[pl.ds(r0, rc), :]
            if final_norm:
                h = _rms_rows(h, g2_ref[...])
            o_ref[pl.ds(r0, rc), :] = h
            return 0

        lax.fori_loop(0, tm // rc, body, 0)


def _ffn(x, norm_w, wg, wu, wd, norm2_w, *, final_norm):
    m, d = x.shape
    f = wg.shape[1]
    tf = _divisor_tile(f, 256, V7X_LANES)
    tm = _divisor_tile(m, 512, 16)
    n_split = max(1, d // 1024)
    vmem = 2 * tm * d * 4 * 2 + tm * d * 2 + 6 * (d * tf * 2) + tm * tf * 4 * 4
    body = functools.partial(_ffn_body, n_split=n_split, final_norm=final_norm)
    return pl.pallas_call(
        body,
        grid=(m // tm, f // tf),
        in_specs=[
            pl.BlockSpec((tm, d), lambda i, j: (i, 0)),
            pl.BlockSpec((1, d), lambda i, j: (0, 0)),
            pl.BlockSpec((d, tf), lambda i, j: (0, j)),
            pl.BlockSpec((d, tf), lambda i, j: (0, j)),
            pl.BlockSpec((tf, d), lambda i, j: (j, 0)),
            pl.BlockSpec((1, d), lambda i, j: (0, 0)),
        ],
        out_specs=pl.BlockSpec((tm, d), lambda i, j: (i, 0)),
        out_shape=jax.ShapeDtypeStruct((m, d), F32),
        scratch_shapes=[pltpu.VMEM((tm, d), BF16)],
        compiler_params=_params(("parallel", "arbitrary"), vmem),
        name="ffn_swiglu",
    )(x, norm_w.reshape(1, d), wg, wu, wd, norm2_w.reshape(1, d))


def _inproj_body(x_ref, g_ref, w_ref, o_ref, hn_ref):
    tm = x_ref.shape[0]
    rc = _row_chunk(tm)

    @pl.when(pl.program_id(1) == 0)
    def _():
        def body(c, _):
            r0 = pl.multiple_of(c * rc, rc)
            hn_ref[pl.ds(r0, rc), :] = _rms_rows(x_ref[pl.ds(r0, rc), :], g_ref[...]).astype(BF16)
            return 0

        lax.fori_loop(0, tm // rc, body, 0)

    o_ref[...] = jnp.dot(hn_ref[...], w_ref[...], preferred_element_type=F32)


def _inproj(x, norm_w, w):
    m, d = x.shape
    n = w.shape[1]
    tm = _divisor_tile(m, 512, 16)
    tn = _divisor_tile(n, 1024, V7X_LANES)
    vmem = 2 * tm * d * 4 + tm * d * 2 + 2 * d * tn * 2 + 2 * tm * tn * 4
    return pl.pallas_call(
        _inproj_body,
        grid=(m // tm, n // tn),
        in_specs=[
            pl.BlockSpec((tm, d), lambda i, j: (i, 0)),
            pl.BlockSpec((1, d), lambda i, j: (0, 0)),
            pl.BlockSpec((d, tn), lambda i, j: (0, j)),
        ],
        out_specs=pl.BlockSpec((tm, tn), lambda i, j: (i, j)),
        out_shape=jax.ShapeDtypeStruct((m, n), F32),
        scratch_shapes=[pltpu.VMEM((tm, d), BF16)],
        compiler_params=_params(("parallel", "arbitrary"), vmem),
        name="in_proj",
    )(x, norm_w.reshape(1, d), w)


def _group_scan_real(a, b):
    row = lax.broadcasted_iota(jnp.int32, a.shape, 1)
    for dist in (1, 2, 4):
        keep = row >= dist
        a_prev = jnp.where(keep, pltpu.roll(a, dist, axis=1), 1.0)
        b_prev = jnp.where(keep, pltpu.roll(b, dist, axis=1), 0.0)
        b = a * b_prev + b
        a = a * a_prev
    return a, b


def _group_scan_const_complex(xr, xi, lam_ref, ns):
    row = lax.broadcasted_iota(jnp.int32, xr.shape, 1)
    for k, dist in enumerate((1, 2, 4)):
        lr = lam_ref[k:k + 1, :ns]
        li = lam_ref[k:k + 1, ns:]
        keep = row >= dist
        pr = jnp.where(keep, pltpu.roll(xr, dist, axis=1), 0.0)
        pi = jnp.where(keep, pltpu.roll(xi, dist, axis=1), 0.0)
        xr, xi = xr + (lr * pr - li * pi), xi + (lr * pi + li * pr)
    return xr, xi


def _rg_body(u_ref, gate_ref, cw_ref, cb_ref, wax_ref, bax_ref, sp_ref, h0_ref, tail0_ref,
             y_ref, ht_ref, tailt_ref, ext_ref, a_ref, b_ref, hc_ref, *, conv_width):
    t = pl.program_id(2)
    tc, hd = u_ref.shape
    ng = tc // V7X_SUBLANES

    @pl.when(t == 0)
    def _():
        ext_ref[0:8, :] = tail0_ref[...]
        hc_ref[...] = h0_ref[...]

    ext_ref[8:8 + tc, :] = u_ref[...]
    xc = cb_ref[...] + cw_ref[conv_width - 1:conv_width, :] * u_ref[...]
    for k in range(conv_width - 1):
        back = conv_width - 1 - k
        xc = xc + cw_ref[k:k + 1, :] * ext_ref[pl.ds(8 - back, tc), :]
    ext_ref[0:8, :] = ext_ref[tc:tc + 8, :]

    pre = jnp.dot(xc.astype(BF16), wax_ref[...], preferred_element_type=F32) + bax_ref[...]
    r = _sigmoid(pre[:, :hd])
    i = _sigmoid(pre[:, hd:])
    log_a = (-RG_C) * r * sp_ref[...]
    a = jnp.exp(log_a)
    b = jnp.sqrt(1.0 - jnp.exp(2.0 * log_a)) * i * xc
    a3, b3 = _group_scan_real(a.reshape(ng, 8, hd), b.reshape(ng, 8, hd))
    a_ref[...] = a3.reshape(tc, hd)
    b_ref[...] = b3.reshape(tc, hd)
    y_ref[...] = _gelu_tanh(gate_ref[...])

    def body(g, hc):
        r0 = pl.multiple_of(g * 8, 8)
        h = a_ref[pl.ds(r0, 8), :] * hc + b_ref[pl.ds(r0, 8), :]
        y_ref[pl.ds(r0, 8), :] = h * y_ref[pl.ds(r0, 8), :]
        return jnp.broadcast_to(h[7:8, :], (8, hd))

    hc = lax.fori_loop(0, ng, body, hc_ref[...])
    hc_ref[...] = hc

    @pl.when(t == pl.num_programs(2) - 1)
    def _():
        ht_ref[...] = hc
        tailt_ref[...] = ext_ref[0:8, :]


def _rg_lru(proj, conv_w, conv_b, wax, bax, softplus_neg_lam, h0, tail0, *, d_rg):
    bsz, t_len, _ = proj.shape
    heads, hd, _ = wax.shape
    conv_width = conv_w.shape[0]
    tc = _divisor_tile(t_len, 512, V7X_SUBLANES)
    vmem = 2 * 3 * tc * hd * 4 + (tc + 8) * hd * 4 + 2 * tc * hd * 4 + 2 * hd * 2 * hd * 2
    body = functools.partial(_rg_body, conv_width=conv_width)
    hsel = lambda b, h, t: (0, 0, h)
    return pl.pallas_call(
        body,
        grid=(bsz, heads, t_len // tc),
        in_specs=[
            pl.BlockSpec((None, tc, hd), lambda b, h, t: (b, t, h)),
            pl.BlockSpec((None, tc, hd), lambda b, h, t: (b, t, heads + h)),
            pl.BlockSpec((conv_width, hd), lambda b, h, t: (0, h)),
            pl.BlockSpec((1, hd), lambda b, h, t: (0, h)),
            pl.BlockSpec((None, hd, 2 * hd), lambda b, h, t: (h, 0, 0)),
            pl.BlockSpec((None, 1, 2 * hd), lambda b, h, t: (h, 0, 0)),
            pl.BlockSpec((1, hd), lambda b, h, t: (0, h)),
            pl.BlockSpec((None, 8, hd), hsel),
            pl.BlockSpec((None, 8, hd), hsel),
        ],
        out_specs=[
            pl.BlockSpec((None, tc, hd), lambda b, h, t: (b, t, h)),
            pl.BlockSpec((None, 8, hd), lambda b, h, t: (b, 0, h)),
            pl.BlockSpec((None, 8, hd), lambda b, h, t: (b, 0, h)),
        ],
        out_shape=[
            jax.ShapeDtypeStruct((bsz, t_len, d_rg), F32),
            jax.ShapeDtypeStruct((bsz, 8, d_rg), F32),
            jax.ShapeDtypeStruct((bsz, 8, d_rg), F32),
        ],
        scratch_shapes=[
            pltpu.VMEM((tc + 8, hd), F32),
            pltpu.VMEM((tc, hd), F32),
            pltpu.VMEM((tc, hd), F32),
            pltpu.VMEM((8, hd), F32),
        ],
        compiler_params=_params(("parallel", "parallel", "arbitrary"), vmem),
        name="rg_lru",
    )(proj, proj, conv_w, conv_b.reshape(1, d_rg), wax, bax, softplus_neg_lam.reshape(1, d_rg),
      h0, tail0)


def _s5_body(u_ref, mi_ref, ws_ref, v_ref, lam_ref, p8_ref, d_ref, x0_ref, z_ref, xt_ref,
             s_ref, xc_ref):
    t = pl.program_id(2)
    tc, slab = u_ref.shape
    nc = tc // S5_CHUNK
    ns = lam_ref.shape[1] // 2
    ng = nc // V7X_SUBLANES

    @pl.when(t == 0)
    def _():
        xc_ref[...] = x0_ref[...]

    us = [u_ref[pl.ds(i, nc, stride=S5_CHUNK), :] for i in range(S5_CHUNK)]
    ucat = jnp.concatenate([ui.astype(BF16) for ui in us], axis=-1)
    ycat = jnp.dot(ucat, mi_ref[...], preferred_element_type=F32)
    send = jnp.dot(ucat, ws_ref[...], preferred_element_type=F32)

    xr, xi = _group_scan_const_complex(send[:, :ns].reshape(ng, 8, ns), send[:, ns:].reshape(ng, 8, ns),
                                       lam_ref, ns)
    s_ref[7:8, :] = xc_ref[0:1, :]
    s_ref[8:8 + nc, :ns] = xr.reshape(nc, ns)
    s_ref[8:8 + nc, ns:] = xi.reshape(nc, ns)

    p8r = p8_ref[:, :ns]
    p8i = p8_ref[:, ns:]

    def body(g, carry):
        cr, ci = carry
        r0 = pl.multiple_of(8 + g * 8, 8)
        nr = s_ref[pl.ds(r0, 8), :ns] + (p8r * cr - p8i * ci)
        ni = s_ref[pl.ds(r0, 8), ns:] + (p8r * ci + p8i * cr)
        s_ref[pl.ds(r0, 8), :ns] = nr
        s_ref[pl.ds(r0, 8), ns:] = ni
        return (jnp.broadcast_to(nr[7:8, :], (8, ns)), jnp.broadcast_to(ni[7:8, :], (8, ns)))

    cr, ci = lax.fori_loop(0, ng, body, (xc_ref[:, :ns], xc_ref[:, ns:]))
    xc_ref[:, :ns] = cr
    xc_ref[:, ns:] = ci

    x_in = s_ref[pl.ds(7, nc), :]
    ycat = ycat + jnp.dot(x_in.astype(BF16), v_ref[...], preferred_element_type=F32)
    for i in range(S5_CHUNK):
        y = ycat[:, i * slab:(i + 1) * slab] + d_ref[...] * us[i]
        z_ref[pl.ds(i, nc, stride=S5_CHUNK), :] = _gelu_tanh(y)

    @pl.when(t == pl.num_programs(2) - 1)
    def _():
        xt_ref[...] = xc_ref[...]


def _s5_scan(proj, m_intra, w_state, v_out, lam_pow, p8, d_skip, x0, *, col0, d_s5):
    bsz, t_len, _ = proj.shape
    nslab, kdim, ns2 = w_state.shape
    slab = kdim // S5_CHUNK
    tc = _divisor_tile(t_len, 4096, S5_CHUNK * V7X_SUBLANES)
    assert tc % (S5_CHUNK * V7X_SUBLANES) == 0, (t_len, tc)
    nc = tc // S5_CHUNK
    cb0 = col0 // slab
    vmem = (4 * tc * slab * 4 + 2 * (2 * kdim * kdim + kdim * ns2) * 2 + (nc + 8) * ns2 * 4
            + nc * (2 * kdim * 4 + kdim * 2 + 3 * ns2 * 4))
    return pl.pallas_call(
        _s5_body,
        grid=(nslab, bsz, t_len // tc),
        in_specs=[
            pl.BlockSpec((None, tc, slab), lambda s, b, t: (b, t, cb0 + s)),
            pl.BlockSpec((None, kdim, kdim), lambda s, b, t: (s, 0, 0)),
            pl.BlockSpec((None, kdim, ns2), lambda s, b, t: (s, 0, 0)),
            pl.BlockSpec((None, ns2, kdim), lambda s, b, t: (s, 0, 0)),
            pl.BlockSpec((None, 8, ns2), lambda s, b, t: (s, 0, 0)),
            pl.BlockSpec((None, 8, ns2), lambda s, b, t: (s, 0, 0)),
            pl.BlockSpec((1, slab), lambda s, b, t: (0, s)),
            pl.BlockSpec((None, None, 8, ns2), lambda s, b, t: (0, s, 0, 0)),
        ],
        out_specs=[
            pl.BlockSpec((None, tc, slab), lambda s, b, t: (b, t, s)),
            pl.BlockSpec((None, None, 8, ns2), lambda s, b, t: (b, s, 0, 0)),
        ],
        out_shape=[
            jax.ShapeDtypeStruct((bsz, t_len, d_s5), F32),
            jax.ShapeDtypeStruct((bsz, nslab, 8, ns2), F32),
        ],
        scratch_shapes=[pltpu.VMEM((nc + 8, ns2), F32), pltpu.VMEM((8, ns2), F32)],
        compiler_params=_params(("parallel", "parallel", "arbitrary"), vmem),
        name="s5_scan",
    )(proj, m_intra, w_state, v_out, lam_pow, p8, d_skip.reshape(1, d_s5), x0)


def _glu_body(z_ref, zc_ref, w_ref, b_ref, o_ref, zb_ref):
    tm = z_ref.shape[0]
    rc = _row_chunk(tm)

    @pl.when(pl.program_id(1) == 0)
    def _():
        def body(c, _):
            r0 = pl.multiple_of(c * rc, rc)
            zb_ref[pl.ds(r0, rc), :] = z_ref[pl.ds(r0, rc), :].astype(BF16)
            return 0

        lax.fori_loop(0, tm // rc, body, 0)

    pre = jnp.dot(zb_ref[...], w_ref[...], preferred_element_type=F32) + b_ref[...]
    o_ref[...] = zc_ref[...] * _sigmoid(pre)


def _glu(z, w, b):
    m, d = z.shape
    tm = _divisor_tile(m, 1024, 16)
    tn = _divisor_tile(d, 512, V7X_LANES)
    vmem = 2 * tm * d * 4 + tm * d * 2 + 2 * d * tn * 2 + 4 * tm * tn * 4
    return pl.pallas_call(
        _glu_body,
        grid=(m // tm, d // tn),
        in_specs=[
            pl.BlockSpec((tm, d), lambda i, j: (i, 0)),
            pl.BlockSpec((tm, tn), lambda i, j: (i, j)),
            pl.BlockSpec((d, tn), lambda i, j: (0, j)),
            pl.BlockSpec((1, tn), lambda i, j: (0, j)),
        ],
        out_specs=pl.BlockSpec((tm, tn), lambda i, j: (i, j)),
        out_shape=jax.ShapeDtypeStruct((m, d), F32),
        scratch_shapes=[pltpu.VMEM((tm, d), BF16)],
        compiler_params=_params(("parallel", "arbitrary"), vmem),
        name="s5_glu",
    )(z, z, w, b.reshape(1, d))


def _outproj_body(yr_ref, ys_ref, gr_ref, gs_ref, w_ref, h_ref, o_ref, yn_ref):
    tm, d_rg = yr_ref.shape
    rc = _row_chunk(tm)

    @pl.when(pl.program_id(1) == 0)
    def _():
        def body(c, _):
            r0 = pl.multiple_of(c * rc, rc)
            yn_ref[pl.ds(r0, rc), :d_rg] = _rms_rows(yr_ref[pl.ds(r0, rc), :], gr_ref[...]).astype(BF16)
            yn_ref[pl.ds(r0, rc), d_rg:] = _rms_rows(ys_ref[pl.ds(r0, rc), :], gs_ref[...]).astype(BF16)
            return 0

        lax.fori_loop(0, tm // rc, body, 0)

    o_ref[...] = h_ref[...] + jnp.dot(yn_ref[...], w_ref[...], preferred_element_type=F32)


def _outproj(y_rg, y_s5, g_rg, g_s5, w, h):
    m, d_rg = y_rg.shape
    d_s5 = y_s5.shape[1]
    d_mix, d = w.shape
    tm = _divisor_tile(m, 512, 16)
    tn = _divisor_tile(d, 1024, V7X_LANES)
    vmem = 2 * tm * d_mix * 4 + tm * d_mix * 2 + 2 * d_mix * tn * 2 + 4 * tm * tn * 4
    return pl.pallas_call(
        _outproj_body,
        grid=(m // tm, d // tn),
        in_specs=[
            pl.BlockSpec((tm, d_rg), lambda i, j: (i, 0)),
            pl.BlockSpec((tm, d_s5), lambda i, j: (i, 0)),
            pl.BlockSpec((1, d_rg), lambda i, j: (0, 0)),
            pl.BlockSpec((1, d_s5), lambda i, j: (0, 0)),
            pl.BlockSpec((d_mix, tn), lambda i, j: (0, j)),
            pl.BlockSpec((tm, tn), lambda i, j: (i, j)),
        ],
        out_specs=pl.BlockSpec((tm, tn), lambda i, j: (i, j)),
        out_shape=jax.ShapeDtypeStruct((m, d), F32),
        scratch_shapes=[pltpu.VMEM((tm, d_mix), BF16)],
        compiler_params=_params(("parallel", "arbitrary"), vmem),
        name="out_proj",
    )(y_rg, y_s5, g_rg.reshape(1, d_rg), g_s5.reshape(1, d_s5), w, h)


def _s5_tables(lam_re, lam_im, log_dt, b_re, b_im, c_re, c_im, slab):
    g, n = lam_re.shape
    c = b_re.shape[-1]
    gs = slab // c
    nslab = g // gs
    ell = S5_CHUNK
    dt = jnp.exp(log_dt.astype(F32))[:, None]
    lam = lax.complex(lam_re.astype(F32), lam_im.astype(F32))
    lam_dt = lam * dt
    lam_bar = jnp.exp(lam_dt)
    b_bar = ((lam_bar - 1.0) / lam)[..., None] * lax.complex(b_re.astype(F32), b_im.astype(F32))
    cc = lax.complex(c_re.astype(F32), c_im.astype(F32))

    def lam_pow(ks):
        return jnp.exp(lam_dt[None] * jnp.asarray(ks, F32)[:, None, None])

    def state_lanes(p):
        p = p.reshape(p.shape[0], nslab, gs * n)
        return jnp.concatenate([jnp.real(p), jnp.imag(p)], axis=-1).transpose(1, 0, 2)

    lam_tab = state_lanes(lam_pow([ell, 2 * ell, 4 * ell] + [ell] * 5))
    p8 = state_lanes(lam_pow([ell * (r + 1) for r in range(8)]))

    eye = jnp.eye(gs, dtype=F32)
    pw = lam_pow(list(range(ell + 1)))
    kk = jnp.real(jnp.einsum('gdn,kgn,gnc->kgdc', cc, pw[:ell], b_bar))
    i_in = jnp.arange(ell)[:, None]
    i_out = jnp.arange(ell)[None, :]
    lag = i_out - i_in
    kt = jnp.where((lag >= 0)[:, :, None, None, None], kk[jnp.clip(lag, 0, ell - 1)], 0.0)
    kt = kt.reshape(ell, ell, nslab, gs, c, c)
    m_intra = jnp.einsum('ijsgdc,gh->sigcjhd', kt, eye).reshape(nslab, ell * gs * c, ell * gs * c)
    wst = pw[ell - 1 - jnp.arange(ell)][..., None] * b_bar[None]
    wst = wst.reshape(ell, nslab, gs, n, c)
    wst = jnp.einsum('isgnc,gh->sigchn', wst, eye.astype(wst.dtype)).reshape(nslab, ell * gs * c, gs * n)
    w_state = jnp.concatenate([jnp.real(wst), jnp.imag(wst)], axis=-1)
    gg = cc[None] * pw[1:ell + 1][:, :, None, :]
    gg = gg.reshape(ell, nslab, gs, c, n)
    gg = jnp.einsum('isgcn,gh->sgnihc', gg, eye.astype(gg.dtype)).reshape(nslab, gs * n, ell * gs * c)
    v_out = jnp.concatenate([jnp.real(gg), -jnp.imag(gg)], axis=1)
    return m_intra.astype(BF16), w_state.astype(BF16), v_out.astype(BF16), lam_tab, p8


def kernel(x, meta_tokens, ffn1_norm, ffn1_w_gate, ffn1_w_up, ffn1_w_down, mix_norm, w_in,
           rg_conv_w, rg_conv_b, rg_w_a, rg_b_a, rg_w_x, rg_b_x, rg_lambda,
           s5_lambda_re, s5_lambda_im, s5_log_dt, s5_b_re, s5_b_im, s5_c_re, s5_c_im, s5_d,
           s5_glu_w, s5_glu_b, rg_out_norm, s5_out_norm, w_out,
           ffn2_norm, ffn2_w_gate, ffn2_w_up, ffn2_w_down, final_norm):
    bsz, t_len, d = x.shape
    depth = ffn1_norm.shape[0]
    d_rg = rg_lambda.shape[-1]
    d_s5 = s5_d.shape[-1]
    heads, hd = rg_w_a.shape[1], rg_w_a.shape[2]
    slab = V7X_LANES
    s5_align = S5_CHUNK * V7X_SUBLANES

    h = x.reshape(bsz * t_len, d)
    hm = meta_tokens.astype(x.dtype)
    n_meta = hm.shape[0]
    for l in range(depth):
        last = l == depth - 1
        ffn1_w = (ffn1_w_gate[l].astype(BF16), ffn1_w_up[l].astype(BF16), ffn1_w_down[l].astype(BF16))
        ffn2_w = (ffn2_w_gate[l].astype(BF16), ffn2_w_up[l].astype(BF16), ffn2_w_down[l].astype(BF16))
        w_in_b = w_in[l].astype(BF16)
        w_out_b = w_out[l].astype(BF16)
        glu_w_b = s5_glu_w[l].astype(BF16)
        wax = jnp.concatenate([rg_w_a[l], rg_w_x[l]], axis=-1).astype(BF16)
        bax = jnp.concatenate([rg_b_a[l], rg_b_x[l]], axis=-1).reshape(heads, 1, 2 * hd)
        sp = jax.nn.softplus(-rg_lambda[l].astype(F32))
        s5_mats = _s5_tables(s5_lambda_re[l], s5_lambda_im[l], s5_log_dt[l], s5_b_re[l], s5_b_im[l],
                             s5_c_re[l], s5_c_im[l], slab)
        nslab, _, ns2 = s5_mats[1].shape
        fnorm = final_norm if last else ffn2_norm[l]

        def mixers(hrows, nb, h0, tail0, x0, front_pad):
            h1 = _ffn(hrows, ffn1_norm[l], *ffn1_w, ffn1_norm[l], final_norm=False)
            proj = _inproj(h1, mix_norm[l], w_in_b).reshape(nb, hrows.shape[0] // nb, -1)
            y_rg, h_t, tail_t = _rg_lru(proj, rg_conv_w[l], rg_conv_b[l], wax, bax, sp, h0, tail0, d_rg=d_rg)
            proj_s5 = jnp.pad(proj, ((0, 0), (front_pad, 0), (0, 0))) if front_pad else proj
            z, x_t = _s5_scan(proj_s5, *s5_mats, s5_d[l], x0, col0=2 * d_rg, d_s5=d_s5)
            return h1, y_rg, z[:, front_pad:], h_t, tail_t, x_t

        def tail_layers(h1, y_rg, z):
            m = h1.shape[0]
            y_s5 = _glu(z.reshape(m, d_s5), glu_w_b, s5_glu_b[l])
            h2 = _outproj(y_rg.reshape(m, d_rg), y_s5, rg_out_norm[l], s5_out_norm[l], w_out_b, h1)
            return _ffn(h2, ffn2_norm[l], *ffn2_w, fnorm, final_norm=last)

        assert t_len % s5_align == 0, t_len
        zeros_rg = jnp.zeros((1, 8, d_rg), F32)
        zeros_s5 = jnp.zeros((1, nslab, 8, ns2), F32)
        meta_pad = (-n_meta) % s5_align
        h1m, y_rgm, zm, h_t, tail_t, x_t = mixers(hm, 1, zeros_rg, zeros_rg, zeros_s5, meta_pad)
        h1, y_rg, z, _, _, _ = mixers(h, bsz, h_t, tail_t, x_t, 0)
        h = tail_layers(h1, y_rg, z)
        if not last:
            hm = tail_layers(h1m, y_rgm, zm)
    return h.reshape(bsz, t_len, d)
```

```python
import functools
import math

import jax
import jax.numpy as jnp
from jax import lax
from jax.experimental import pallas as pl
from jax.experimental.pallas import tpu as pltpu

EPS = 1e-6
RG_C = 8.0
F32 = jnp.float32
BF16 = jnp.bfloat16

V7X_LANES = 128
V7X_SUBLANES = 8
V7X_VMEM_BYTES = 64 * 1024 * 1024

S5_CHUNK = 8


def _divisor_tile(n, pref, align):
    if n <= pref:
        return n
    t = (pref // align) * align
    while t >= align:
        if n % t == 0:
            return t
        t -= align
    return n


def _params(sem, vmem_bytes):
    limit = int(min(V7X_VMEM_BYTES - (2 << 20), max(vmem_bytes + (6 << 20), 32 << 20)))
    return pltpu.CompilerParams(dimension_semantics=sem, vmem_limit_bytes=limit)


def _rms_rows(x, g):
    ms = jnp.mean(x * x, axis=-1, keepdims=True)
    return x * lax.rsqrt(ms + EPS) * g


def _sigmoid(x):
    return 1.0 / (1.0 + jnp.exp(-x))


def _gelu_tanh(x):
    c = math.sqrt(2.0 / math.pi)
    return 0.5 * x * (1.0 + jnp.tanh(c * (x + 0.044715 * (x * x * x))))


def _row_chunk(tm):
    return _divisor_tile(tm, 32, V7X_SUBLANES)


def _ffn_body(x_ref, g_ref, wg_ref, wu_ref, wd_ref, g2_ref, o_ref, hn_ref, *, n_split, final_norm):
    j = pl.program_id(1)
    tm, d = x_ref.shape
    rc = _row_chunk(tm)

    @pl.when(j == 0)
    def _():
        def body(c, _):
            r0 = pl.multiple_of(c * rc, rc)
            x = x_ref[pl.ds(r0, rc), :]
            hn_ref[pl.ds(r0, rc), :] = _rms_rows(x, g_ref[...]).astype(BF16)
            return 0

        lax.fori_loop(0, tm // rc, body, 0, unroll=2)
        o_ref[...] = jnp.zeros_like(o_ref)

    g = jnp.dot(hn_ref[...], wg_ref[...], preferred_element_type=F32)
    u = jnp.dot(hn_ref[...], wu_ref[...], preferred_element_type=F32)
    a = (g * _sigmoid(g) * u).astype(BF16)
    dn = d // n_split
    for s in range(n_split):
        o_ref[:, s * dn:(s + 1) * dn] += jnp.dot(a, wd_ref[:, s * dn:(s + 1) * dn],
                                                 preferred_element_type=F32)

    @pl.when(j == pl.num_programs(1) - 1)
    def _():
        def body(c, _):
            r0 = pl.multiple_of(c * rc, rc)
            h = x_ref[pl.ds(r0, rc), :] + 0.5 * o_ref[pl.ds(r0, rc), :]
            if final_norm:
                h = _rms_rows(h, g2_ref[...])
            o_ref[pl.ds(r0, rc), :] = h
            return 0

        lax.fori_loop(0, tm // rc, body, 0, unroll=2)


def _ffn(x, norm_w, wg, wu, wd, norm2_w, *, final_norm):
    m, d = x.shape
    f = wg.shape[1]
    tf = _divisor_tile(f, 256, V7X_LANES)
    tm = _divisor_tile(m, 512, 16)
    n_split = max(1, d // 1024)
    vmem = 2 * tm * d * 4 * 2 + tm * d * 2 + 6 * (d * tf * 2) + tm * tf * 4 * 4
    body = functools.partial(_ffn_body, n_split=n_split, final_norm=final_norm)
    return pl.pallas_call(
        body,
        grid=(m // tm, f // tf),
        in_specs=[
            pl.BlockSpec((tm, d), lambda i, j: (i, 0)),
            pl.BlockSpec((1, d), lambda i, j: (0, 0)),
            pl.BlockSpec((d, tf), lambda i, j: (0, j)),
            pl.BlockSpec((d, tf), lambda i, j: (0, j)),
            pl.BlockSpec((tf, d), lambda i, j: (j, 0)),
            pl.BlockSpec((1, d), lambda i, j: (0, 0)),
        ],
        out_specs=pl.BlockSpec((tm, d), lambda i, j: (i, 0)),
        out_shape=jax.ShapeDtypeStruct((m, d), F32),
        scratch_shapes=[pltpu.VMEM((tm, d), BF16)],
        compiler_params=_params(("parallel", "arbitrary"), vmem),
        name="ffn_swiglu",
    )(x, norm_w.reshape(1, d), wg, wu, wd, norm2_w.reshape(1, d))


def _inproj_body(x_ref, g_ref, w_ref, o_ref, hn_ref):
    tm = x_ref.shape[0]
    rc = _row_chunk(tm)

    @pl.when(pl.program_id(1) == 0)
    def _():
        def body(c, _):
            r0 = pl.multiple_of(c * rc, rc)
            hn_ref[pl.ds(r0, rc), :] = _rms_rows(x_ref[pl.ds(r0, rc), :], g_ref[...]).astype(BF16)
            return 0

        lax.fori_loop(0, tm // rc, body, 0, unroll=2)

    o_ref[...] = jnp.dot(hn_ref[...], w_ref[...], preferred_element_type=F32)


def _inproj(x, norm_w, w):
    m, d = x.shape
    n = w.shape[1]
    tm = _divisor_tile(m, 512, 16)
    tn = _divisor_tile(n, 1024, V7X_LANES)
    vmem = 2 * tm * d * 4 + tm * d * 2 + 2 * d * tn * 2 + 2 * tm * tn * 4
    return pl.pallas_call(
        _inproj_body,
        grid=(m // tm, n // tn),
        in_specs=[
            pl.BlockSpec((tm, d), lambda i, j: (i, 0)),
            pl.BlockSpec((1, d), lambda i, j: (0, 0)),
            pl.BlockSpec((d, tn), lambda i, j: (0, j)),
        ],
        out_specs=pl.BlockSpec((tm, tn), lambda i, j: (i, j)),
        out_shape=jax.ShapeDtypeStruct((m, n), F32),
        scratch_shapes=[pltpu.VMEM((tm, d), BF16)],
        compiler_params=_params(("parallel", "arbitrary"), vmem),
        name="in_proj",
    )(x, norm_w.reshape(1, d), w)


def _group_scan_real(a, b):
    row = lax.broadcasted_iota(jnp.int32, a.shape, 1)
    for dist in (1, 2, 4):
        keep = row >= dist
        a_prev = jnp.where(keep, pltpu.roll(a, dist, axis=1), 1.0)
        b_prev = jnp.where(keep, pltpu.roll(b, dist, axis=1), 0.0)
        b = a * b_prev + b
        a = a * a_prev
    return a, b


def _group_scan_const_complex(xr, xi, lam_ref, ns):
    row = lax.broadcasted_iota(jnp.int32, xr.shape, 1)
    for k, dist in enumerate((1, 2, 4)):
        lr = lam_ref[k:k + 1, :ns]
        li = lam_ref[k:k + 1, ns:]
        keep = row >= dist
        pr = jnp.where(keep, pltpu.roll(xr, dist, axis=1), 0.0)
        pi = jnp.where(keep, pltpu.roll(xi, dist, axis=1), 0.0)
        xr, xi = xr + (lr * pr - li * pi), xi + (lr * pi + li * pr)
    return xr, xi


def _rg_body(u_ref, gate_ref, cw_ref, cb_ref, wax_ref, bax_ref, sp_ref, h0_ref, tail0_ref,
             y_ref, ht_ref, tailt_ref, ext_ref, a_ref, b_ref, hc_ref, *, conv_width):
    t = pl.program_id(2)
    tc, width = u_ref.shape
    hpc, hd, _ = wax_ref.shape
    ng = tc // V7X_SUBLANES

    @pl.when(t == 0)
    def _():
        ext_ref[0:8, :] = tail0_ref[...]
        hc_ref[...] = h0_ref[...]

    ext_ref[8:8 + tc, :] = u_ref[...]
    for hh in range(hpc):
        cols = slice(hh * hd, (hh + 1) * hd)
        xc = cb_ref[:, cols] + cw_ref[conv_width - 1:conv_width, cols] * u_ref[:, cols]
        for k in range(conv_width - 1):
            back = conv_width - 1 - k
            xc = xc + cw_ref[k:k + 1, cols] * ext_ref[pl.ds(8 - back, tc), cols]
        pre = jnp.dot(xc.astype(BF16), wax_ref[hh], preferred_element_type=F32) + bax_ref[hh]
        r = _sigmoid(pre[:, :hd])
        i = _sigmoid(pre[:, hd:])
        a = jnp.exp((-RG_C) * r * sp_ref[:, cols])
        b = jnp.sqrt(1.0 - a * a) * i * xc
        a3, b3 = _group_scan_real(a.reshape(ng, 8, hd), b.reshape(ng, 8, hd))
        a_ref[:, cols] = a3.reshape(tc, hd)
        b_ref[:, cols] = b3.reshape(tc, hd)
    ext_ref[0:8, :] = ext_ref[tc:tc + 8, :]
    y_ref[...] = _gelu_tanh(gate_ref[...])

    def body(g, hc):
        r0 = pl.multiple_of(g * 8, 8)
        h = a_ref[pl.ds(r0, 8), :] * hc + b_ref[pl.ds(r0, 8), :]
        y_ref[pl.ds(r0, 8), :] = h * y_ref[pl.ds(r0, 8), :]
        return jnp.broadcast_to(h[7:8, :], (8, width))

    hc = lax.fori_loop(0, ng, body, hc_ref[...], unroll=2)
    hc_ref[...] = hc

    @pl.when(t == pl.num_programs(2) - 1)
    def _():
        ht_ref[...] = hc
        tailt_ref[...] = ext_ref[0:8, :]


def _rg_lru(proj, conv_w, conv_b, wax, bax, softplus_neg_lam, h0, tail0, *, d_rg):
    bsz, t_len, _ = proj.shape
    heads, hd, _ = wax.shape
    conv_width = conv_w.shape[0]
    tc = _divisor_tile(t_len, 512, V7X_SUBLANES)
    hpc = _divisor_tile(heads, max(1, 1024 // hd), 1)
    wd = hpc * hd
    ncell = heads // hpc
    vmem = 2 * 3 * tc * wd * 4 + (tc + 8) * wd * 4 + 2 * tc * wd * 4 + 2 * hpc * hd * 2 * hd * 2 + 8 * tc * hd * 4
    body = functools.partial(_rg_body, conv_width=conv_width)
    hsel = lambda b, h, t: (0, 0, h)
    return pl.pallas_call(
        body,
        grid=(bsz, ncell, t_len // tc),
        in_specs=[
            pl.BlockSpec((None, tc, wd), lambda b, h, t: (b, t, h)),
            pl.BlockSpec((None, tc, wd), lambda b, h, t: (b, t, ncell + h)),
            pl.BlockSpec((conv_width, wd), lambda b, h, t: (0, h)),
            pl.BlockSpec((1, wd), lambda b, h, t: (0, h)),
            pl.BlockSpec((hpc, hd, 2 * hd), lambda b, h, t: (h, 0, 0)),
            pl.BlockSpec((hpc, 1, 2 * hd), lambda b, h, t: (h, 0, 0)),
            pl.BlockSpec((1, wd), lambda b, h, t: (0, h)),
            pl.BlockSpec((None, 8, wd), hsel),
            pl.BlockSpec((None, 8, wd), hsel),
        ],
        out_specs=[
            pl.BlockSpec((None, tc, wd), lambda b, h, t: (b, t, h)),
            pl.BlockSpec((None, 8, wd), lambda b, h, t: (b, 0, h)),
            pl.BlockSpec((None, 8, wd), lambda b, h, t: (b, 0, h)),
        ],
        out_shape=[
            jax.ShapeDtypeStruct((bsz, t_len, d_rg), F32),
            jax.ShapeDtypeStruct((bsz, 8, d_rg), F32),
            jax.ShapeDtypeStruct((bsz, 8, d_rg), F32),
        ],
        scratch_shapes=[
            pltpu.VMEM((tc + 8, wd), F32),
            pltpu.VMEM((tc, wd), F32),
            pltpu.VMEM((tc, wd), F32),
            pltpu.VMEM((8, wd), F32),
        ],
        compiler_params=_params(("parallel", "parallel", "arbitrary"), vmem),
        name="rg_lru",
    )(proj, proj, conv_w, conv_b.reshape(1, d_rg), wax, bax, softplus_neg_lam.reshape(1, d_rg),
      h0, tail0)


def _s5_body(u_ref, mi_ref, ws_ref, v_ref, lam_ref, p8_ref, d_ref, x0_ref, z_ref, xt_ref,
             s_ref, xc_ref):
    t = pl.program_id(2)
    tc, slab = u_ref.shape
    nc = tc // S5_CHUNK
    ns = lam_ref.shape[1] // 2
    ng = nc // V7X_SUBLANES

    @pl.when(t == 0)
    def _():
        xc_ref[...] = x0_ref[...]

    us = [u_ref[pl.ds(i, nc, stride=S5_CHUNK), :] for i in range(S5_CHUNK)]
    ucat = jnp.concatenate([ui.astype(BF16) for ui in us], axis=-1)
    send = jnp.dot(ucat, ws_ref[...], preferred_element_type=F32)
    ycat = jnp.dot(ucat, mi_ref[...], preferred_element_type=F32)

    xr, xi = _group_scan_const_complex(send[:, :ns].reshape(ng, 8, ns), send[:, ns:].reshape(ng, 8, ns),
                                       lam_ref, ns)
    s_ref[7:8, :] = xc_ref[0:1, :]
    s_ref[8:8 + nc, :ns] = xr.reshape(nc, ns)
    s_ref[8:8 + nc, ns:] = xi.reshape(nc, ns)

    p8r = p8_ref[:, :ns]
    p8i = p8_ref[:, ns:]

    def body(g, carry):
        cr, ci = carry
        r0 = pl.multiple_of(8 + g * 8, 8)
        nr = s_ref[pl.ds(r0, 8), :ns] + (p8r * cr - p8i * ci)
        ni = s_ref[pl.ds(r0, 8), ns:] + (p8r * ci + p8i * cr)
        s_ref[pl.ds(r0, 8), :ns] = nr
        s_ref[pl.ds(r0, 8), ns:] = ni
        return (jnp.broadcast_to(nr[7:8, :], (8, ns)), jnp.broadcast_to(ni[7:8, :], (8, ns)))

    cr, ci = lax.fori_loop(0, ng, body, (xc_ref[:, :ns], xc_ref[:, ns:]), unroll=True)
    xc_ref[:, :ns] = cr
    xc_ref[:, ns:] = ci

    x_in = s_ref[pl.ds(7, nc), :]
    ycat = ycat + jnp.dot(x_in.astype(BF16), v_ref[...], preferred_element_type=F32)
    for i in range(S5_CHUNK):
        y = ycat[:, i * slab:(i + 1) * slab] + d_ref[...] * us[i]
        z_ref[pl.ds(i, nc, stride=S5_CHUNK), :] = _gelu_tanh(y)

    @pl.when(t == pl.num_programs(2) - 1)
    def _():
        xt_ref[...] = xc_ref[...]


def _s5_scan(proj, m_intra, w_state, v_out, lam_pow, p8, d_skip, x0, *, col0, d_s5):
    bsz, t_len, _ = proj.shape
    nslab, kdim, ns2 = w_state.shape
    slab = kdim // S5_CHUNK
    tc = _divisor_tile(t_len, 4096, S5_CHUNK * V7X_SUBLANES)
    assert tc % (S5_CHUNK * V7X_SUBLANES) == 0, (t_len, tc)
    nc = tc // S5_CHUNK
    cb0 = col0 // slab
    vmem = (4 * tc * slab * 4 + 2 * (2 * kdim * kdim + kdim * ns2) * 2 + (nc + 8) * ns2 * 4
            + nc * (2 * kdim * 4 + kdim * 2 + 3 * ns2 * 4))
    return pl.pallas_call(
        _s5_body,
        grid=(nslab, bsz, t_len // tc),
        in_specs=[
            pl.BlockSpec((None, tc, slab), lambda s, b, t: (b, t, cb0 + s)),
            pl.BlockSpec((None, kdim, kdim), lambda s, b, t: (s, 0, 0)),
            pl.BlockSpec((None, kdim, ns2), lambda s, b, t: (s, 0, 0)),
            pl.BlockSpec((None, ns2, kdim), lambda s, b, t: (s, 0, 0)),
            pl.BlockSpec((None, 8, ns2), lambda s, b, t: (s, 0, 0)),
            pl.BlockSpec((None, 8, ns2), lambda s, b, t: (s, 0, 0)),
            pl.BlockSpec((1, slab), lambda s, b, t: (0, s)),
            pl.BlockSpec((None, None, 8, ns2), lambda s, b, t: (0, s, 0, 0)),
        ],
        out_specs=[
            pl.BlockSpec((None, tc, slab), lambda s, b, t: (b, t, s)),
            pl.BlockSpec((None, None, 8, ns2), lambda s, b, t: (b, s, 0, 0)),
        ],
        out_shape=[
            jax.ShapeDtypeStruct((bsz, t_len, d_s5), F32),
            jax.ShapeDtypeStruct((bsz, nslab, 8, ns2), F32),
        ],
        scratch_shapes=[pltpu.VMEM((nc + 8, ns2), F32), pltpu.VMEM((8, ns2), F32)],
        compiler_params=_params(("parallel", "parallel", "arbitrary"), vmem),
        name="s5_scan",
    )(proj, m_intra, w_state, v_out, lam_pow, p8, d_skip.reshape(1, d_s5), x0)


def _glu_body(z_ref, zc_ref, w_ref, b_ref, o_ref, zb_ref):
    tm = z_ref.shape[0]
    rc = _row_chunk(tm)

    @pl.when(pl.program_id(1) == 0)
    def _():
        def body(c, _):
            r0 = pl.multiple_of(c * rc, rc)
            zb_ref[pl.ds(r0, rc), :] = z_ref[pl.ds(r0, rc), :].astype(BF16)
            return 0

        lax.fori_loop(0, tm // rc, body, 0, unroll=2)

    pre = jnp.dot(zb_ref[...], w_ref[...], preferred_element_type=F32) + b_ref[...]
    o_ref[...] = zc_ref[...] * _sigmoid(pre)


def _glu(z, w, b):
    m, d = z.shape
    tm = _divisor_tile(m, 1024, 16)
    tn = _divisor_tile(d, 512, V7X_LANES)
    vmem = 2 * tm * d * 4 + tm * d * 2 + 2 * d * tn * 2 + 4 * tm * tn * 4
    return pl.pallas_call(
        _glu_body,
        grid=(m // tm, d // tn),
        in_specs=[
            pl.BlockSpec((tm, d), lambda i, j: (i, 0)),
            pl.BlockSpec((tm, tn), lambda i, j: (i, j)),
            pl.BlockSpec((d, tn), lambda i, j: (0, j)),
            pl.BlockSpec((1, tn), lambda i, j: (0, j)),
        ],
        out_specs=pl.BlockSpec((tm, tn), lambda i, j: (i, j)),
        out_shape=jax.ShapeDtypeStruct((m, d), F32),
        scratch_shapes=[pltpu.VMEM((tm, d), BF16)],
        compiler_params=_params(("parallel", "arbitrary"), vmem),
        name="s5_glu",
    )(z, z, w, b.reshape(1, d))


def _outproj_body(yr_ref, ys_ref, gr_ref, gs_ref, w_ref, h_ref, o_ref, yn_ref):
    tm, d_rg = yr_ref.shape
    rc = _row_chunk(tm)

    @pl.when(pl.program_id(1) == 0)
    def _():
        def body(c, _):
            r0 = pl.multiple_of(c * rc, rc)
            yn_ref[pl.ds(r0, rc), :d_rg] = _rms_rows(yr_ref[pl.ds(r0, rc), :], gr_ref[...]).astype(BF16)
            yn_ref[pl.ds(r0, rc), d_rg:] = _rms_rows(ys_ref[pl.ds(r0, rc), :], gs_ref[...]).astype(BF16)
            return 0

        lax.fori_loop(0, tm // rc, body, 0, unroll=2)

    o_ref[...] = h_ref[...] + jnp.dot(yn_ref[...], w_ref[...], preferred_element_type=F32)


def _outproj(y_rg, y_s5, g_rg, g_s5, w, h):
    m, d_rg = y_rg.shape
    d_s5 = y_s5.shape[1]
    d_mix, d = w.shape
    tm = _divisor_tile(m, 512, 16)
    tn = _divisor_tile(d, 1024, V7X_LANES)
    vmem = 2 * tm * d_mix * 4 + tm * d_mix * 2 + 2 * d_mix * tn * 2 + 4 * tm * tn * 4
    return pl.pallas_call(
        _outproj_body,
        grid=(m // tm, d // tn),
        in_specs=[
            pl.BlockSpec((tm, d_rg), lambda i, j: (i, 0)),
            pl.BlockSpec((tm, d_s5), lambda i, j: (i, 0)),
            pl.BlockSpec((1, d_rg), lambda i, j: (0, 0)),
            pl.BlockSpec((1, d_s5), lambda i, j: (0, 0)),
            pl.BlockSpec((d_mix, tn), lambda i, j: (0, j)),
            pl.BlockSpec((tm, tn), lambda i, j: (i, j)),
        ],
        out_specs=pl.BlockSpec((tm, tn), lambda i, j: (i, j)),
        out_shape=jax.ShapeDtypeStruct((m, d), F32),
        scratch_shapes=[pltpu.VMEM((tm, d_mix), BF16)],
        compiler_params=_params(("parallel", "arbitrary"), vmem),
        name="out_proj",
    )(y_rg, y_s5, g_rg.reshape(1, d_rg), g_s5.reshape(1, d_s5), w, h)


def _s5_tables(lam_re, lam_im, log_dt, b_re, b_im, c_re, c_im, slab):
    g, n = lam_re.shape
    c = b_re.shape[-1]
    gs = slab // c
    nslab = g // gs
    ell = S5_CHUNK
    dt = jnp.exp(log_dt.astype(F32))[:, None]
    lam = lax.complex(lam_re.astype(F32), lam_im.astype(F32))
    lam_dt = lam * dt
    lam_bar = jnp.exp(lam_dt)
    b_bar = ((lam_bar - 1.0) / lam)[..., None] * lax.complex(b_re.astype(F32), b_im.astype(F32))
    cc = lax.complex(c_re.astype(F32), c_im.astype(F32))

    def lam_pow(ks):
        return jnp.exp(lam_dt[None] * jnp.asarray(ks, F32)[:, None, None])

    def state_lanes(p):
        p = p.reshape(p.shape[0], nslab, gs * n)
        return jnp.concatenate([jnp.real(p), jnp.imag(p)], axis=-1).transpose(1, 0, 2)

    lam_tab = state_lanes(lam_pow([ell, 2 * ell, 4 * ell] + [ell] * 5))
    p8 = state_lanes(lam_pow([ell * (r + 1) for r in range(8)]))

    eye = jnp.eye(gs, dtype=F32)
    pw = lam_pow(list(range(ell + 1)))
    kk = jnp.real(jnp.einsum('gdn,kgn,gnc->kgdc', cc, pw[:ell], b_bar))
    lag = jnp.arange(ell)[None, :] - jnp.arange(ell)[:, None]
    kt = jnp.where((lag >= 0)[:, :, None, None, None], kk[jnp.clip(lag, 0, ell - 1)], 0.0)
    kt = kt.reshape(ell, ell, nslab, gs, c, c).transpose(2, 0, 3, 5, 1, 4)
    m_intra = (kt[:, :, :, :, :, None, :] * eye[None, None, :, None, None, :, None]).astype(BF16)
    m_intra = m_intra.reshape(nslab, ell * gs * c, ell * gs * c)
    wst = pw[ell - 1 - jnp.arange(ell)][..., None] * b_bar[None]
    wst = wst.reshape(ell, nslab, gs, n, c).transpose(1, 0, 2, 4, 3)
    eye_w = eye[None, None, :, None, :, None]
    w_state = jnp.concatenate(
        [(part(wst)[:, :, :, :, None, :] * eye_w).astype(BF16).reshape(nslab, ell * gs * c, gs * n)
         for part in (jnp.real, jnp.imag)], axis=-1)
    gg = cc[None] * pw[1:ell + 1][:, :, None, :]
    gg = gg.reshape(ell, nslab, gs, c, n).transpose(1, 2, 4, 0, 3)
    eye_v = eye[None, :, None, None, :, None]
    v_out = jnp.concatenate(
        [(part[:, :, :, :, None, :] * eye_v).astype(BF16).reshape(nslab, gs * n, ell * gs * c)
         for part in (jnp.real(gg), -jnp.imag(gg))], axis=1)
    return m_intra, w_state, v_out, lam_tab, p8


def kernel(x, meta_tokens, ffn1_norm, ffn1_w_gate, ffn1_w_up, ffn1_w_down, mix_norm, w_in,
           rg_conv_w, rg_conv_b, rg_w_a, rg_b_a, rg_w_x, rg_b_x, rg_lambda,
           s5_lambda_re, s5_lambda_im, s5_log_dt, s5_b_re, s5_b_im, s5_c_re, s5_c_im, s5_d,
           s5_glu_w, s5_glu_b, rg_out_norm, s5_out_norm, w_out,
           ffn2_norm, ffn2_w_gate, ffn2_w_up, ffn2_w_down, final_norm):
    bsz, t_len, d = x.shape
    depth = ffn1_norm.shape[0]
    d_rg = rg_lambda.shape[-1]
    d_s5 = s5_d.shape[-1]
    heads, hd = rg_w_a.shape[1], rg_w_a.shape[2]
    slab = V7X_LANES
    s5_align = S5_CHUNK * V7X_SUBLANES

    h = x.reshape(bsz * t_len, d)
    hm = meta_tokens.astype(x.dtype)
    n_meta = hm.shape[0]
    for l in range(depth):
        last = l == depth - 1
        ffn1_w = (ffn1_w_gate[l].astype(BF16), ffn1_w_up[l].astype(BF16), ffn1_w_down[l].astype(BF16))
        ffn2_w = (ffn2_w_gate[l].astype(BF16), ffn2_w_up[l].astype(BF16), ffn2_w_down[l].astype(BF16))
        w_in_b = w_in[l].astype(BF16)
        w_out_b = w_out[l].astype(BF16)
        glu_w_b = s5_glu_w[l].astype(BF16)
        wax = jnp.concatenate([rg_w_a[l], rg_w_x[l]], axis=-1).astype(BF16)
        bax = jnp.concatenate([rg_b_a[l], rg_b_x[l]], axis=-1).reshape(heads, 1, 2 * hd)
        sp = jax.nn.softplus(-rg_lambda[l].astype(F32))
        s5_mats = _s5_tables(s5_lambda_re[l], s5_lambda_im[l], s5_log_dt[l], s5_b_re[l], s5_b_im[l],
                             s5_c_re[l], s5_c_im[l], slab)
        nslab, _, ns2 = s5_mats[1].shape
        fnorm = final_norm if last else ffn2_norm[l]

        def mixers(hrows, nb, h0, tail0, x0, front_pad):
            h1 = _ffn(hrows, ffn1_norm[l], *ffn1_w, ffn1_norm[l], final_norm=False)
            proj = _inproj(h1, mix_norm[l], w_in_b).reshape(nb, hrows.shape[0] // nb, -1)
            y_rg, h_t, tail_t = _rg_lru(proj, rg_conv_w[l], rg_conv_b[l], wax, bax, sp, h0, tail0, d_rg=d_rg)
            proj_s5 = jnp.pad(proj, ((0, 0), (front_pad, 0), (0, 0))) if front_pad else proj
            z, x_t = _s5_scan(proj_s5, *s5_mats, s5_d[l], x0, col0=2 * d_rg, d_s5=d_s5)
            return h1, y_rg, z[:, front_pad:], h_t, tail_t, x_t

        def tail_layers(h1, y_rg, z):
            m = h1.shape[0]
            y_s5 = _glu(z.reshape(m, d_s5), glu_w_b, s5_glu_b[l])
            h2 = _outproj(y_rg.reshape(m, d_rg), y_s5, rg_out_norm[l], s5_out_norm[l], w_out_b, h1)
            return _ffn(h2, ffn2_norm[l], *ffn2_w, fnorm, final_norm=last)

        assert t_len % s5_align == 0, t_len
        zeros_rg = jnp.zeros((1, 8, d_rg), F32)
        zeros_s5 = jnp.zeros((1, nslab, 8, ns2), F32)
        meta_pad = (-n_meta) % s5_align
        h1m, y_rgm, zm, h_t, tail_t, x_t = mixers(hm, 1, zeros_rg, zeros_rg, zeros_s5, meta_pad)
        h1, y_rg, z, _, _, _ = mixers(h, bsz, h_t, tail_t, x_t, 0)
        h = tail_layers(h1, y_rg, z)
        if not last:
            hm = tail_layers(h1m, y_rgm, zm)
    return h.reshape(bsz, t_len, d)
```

```python
import functools
import math

import jax
import jax.numpy as jnp
import numpy as np
from jax import lax
from jax.experimental import pallas as pl
from jax.experimental.pallas import tpu as pltpu

EPS = 1e-6
RG_C = 8.0
F32 = jnp.float32
BF16 = jnp.bfloat16

V7X_LANES = 128
V7X_SUBLANES = 8
V7X_VMEM_BYTES = 64 * 1024 * 1024

S5_CHUNK = 8


def _divisor_tile(n, pref, align):
    if n <= pref:
        return n
    t = (pref // align) * align
    while t >= align:
        if n % t == 0:
            return t
        t -= align
    return n


def _params(sem, vmem_bytes):
    limit = int(min(V7X_VMEM_BYTES - (2 << 20), max(vmem_bytes + (6 << 20), 32 << 20)))
    return pltpu.CompilerParams(dimension_semantics=sem, vmem_limit_bytes=limit)


def _rms_rows(x, g):
    ms = jnp.mean(x * x, axis=-1, keepdims=True)
    return x * lax.rsqrt(ms + EPS) * g


def _sigmoid(x):
    return 1.0 / (1.0 + jnp.exp(-x))


def _gelu_tanh(x):
    c = math.sqrt(2.0 / math.pi)
    return 0.5 * x * (1.0 + jnp.tanh(c * (x + 0.044715 * (x * x * x))))


def _row_chunk(tm):
    return _divisor_tile(tm, 32, V7X_SUBLANES)


def _ffn_body(x_ref, g_ref, wg_ref, wu_ref, wd_ref, g2_ref, o_ref, hn_ref, *, n_split, final_norm):
    j = pl.program_id(1)
    tm, d = x_ref.shape
    rc = _row_chunk(tm)

    @pl.when(j == 0)
    def _():
        def body(c, _):
            r0 = pl.multiple_of(c * rc, rc)
            x = x_ref[pl.ds(r0, rc), :]
            hn_ref[pl.ds(r0, rc), :] = _rms_rows(x, g_ref[...]).astype(BF16)
            return 0

        lax.fori_loop(0, tm // rc, body, 0, unroll=2)
        o_ref[...] = jnp.zeros_like(o_ref)

    g = jnp.dot(hn_ref[...], wg_ref[...], preferred_element_type=F32)
    u = jnp.dot(hn_ref[...], wu_ref[...], preferred_element_type=F32)
    a = (g * _sigmoid(g) * u).astype(BF16)
    dn = d // n_split
    for s in range(n_split):
        o_ref[:, s * dn:(s + 1) * dn] += jnp.dot(a, wd_ref[:, s * dn:(s + 1) * dn],
                                                 preferred_element_type=F32)

    @pl.when(j == pl.num_programs(1) - 1)
    def _():
        def body(c, _):
            r0 = pl.multiple_of(c * rc, rc)
            h = x_ref[pl.ds(r0, rc), :] + 0.5 * o_ref[pl.ds(r0, rc), :]
            if final_norm:
                h = _rms_rows(h, g2_ref[...])
            o_ref[pl.ds(r0, rc), :] = h
            return 0

        lax.fori_loop(0, tm // rc, body, 0, unroll=2)


def _ffn(x, norm_w, wg, wu, wd, norm2_w, *, final_norm):
    m, d = x.shape
    f = wg.shape[1]
    tf = _divisor_tile(f, 256, V7X_LANES)
    tm = _divisor_tile(m, 512, 16)
    n_split = max(1, d // 1024)
    vmem = 2 * tm * d * 4 * 2 + tm * d * 2 + 6 * (d * tf * 2) + tm * tf * 4 * 4
    body = functools.partial(_ffn_body, n_split=n_split, final_norm=final_norm)
    return pl.pallas_call(
        body,
        grid=(m // tm, f // tf),
        in_specs=[
            pl.BlockSpec((tm, d), lambda i, j: (i, 0)),
            pl.BlockSpec((1, d), lambda i, j: (0, 0)),
            pl.BlockSpec((d, tf), lambda i, j: (0, j)),
            pl.BlockSpec((d, tf), lambda i, j: (0, j)),
            pl.BlockSpec((tf, d), lambda i, j: (j, 0)),
            pl.BlockSpec((1, d), lambda i, j: (0, 0)),
        ],
        out_specs=pl.BlockSpec((tm, d), lambda i, j: (i, 0)),
        out_shape=jax.ShapeDtypeStruct((m, d), F32),
        scratch_shapes=[pltpu.VMEM((tm, d), BF16)],
        compiler_params=_params(("parallel", "arbitrary"), vmem),
        name="ffn_swiglu",
    )(x, norm_w.reshape(1, d), wg, wu, wd, norm2_w.reshape(1, d))


def _inproj_body(x_ref, g_ref, w_ref, o_ref, hn_ref):
    tm = x_ref.shape[0]
    rc = _row_chunk(tm)

    @pl.when(pl.program_id(1) == 0)
    def _():
        def body(c, _):
            r0 = pl.multiple_of(c * rc, rc)
            hn_ref[pl.ds(r0, rc), :] = _rms_rows(x_ref[pl.ds(r0, rc), :], g_ref[...]).astype(BF16)
            return 0

        lax.fori_loop(0, tm // rc, body, 0, unroll=2)

    o_ref[...] = jnp.dot(hn_ref[...], w_ref[...], preferred_element_type=F32)


def _inproj(x, norm_w, w):
    m, d = x.shape
    n = w.shape[1]
    tm = _divisor_tile(m, 512, 16)
    tn = _divisor_tile(n, 1024, V7X_LANES)
    vmem = 2 * tm * d * 4 + tm * d * 2 + 2 * d * tn * 2 + 2 * tm * tn * 4
    return pl.pallas_call(
        _inproj_body,
        grid=(m // tm, n // tn),
        in_specs=[
            pl.BlockSpec((tm, d), lambda i, j: (i, 0)),
            pl.BlockSpec((1, d), lambda i, j: (0, 0)),
            pl.BlockSpec((d, tn), lambda i, j: (0, j)),
        ],
        out_specs=pl.BlockSpec((tm, tn), lambda i, j: (i, j)),
        out_shape=jax.ShapeDtypeStruct((m, n), F32),
        scratch_shapes=[pltpu.VMEM((tm, d), BF16)],
        compiler_params=_params(("parallel", "arbitrary"), vmem),
        name="in_proj",
    )(x, norm_w.reshape(1, d), w)


def _group_scan_real(a, b):
    row = lax.broadcasted_iota(jnp.int32, a.shape, 1)
    for dist in (1, 2, 4):
        keep = row >= dist
        a_prev = jnp.where(keep, pltpu.roll(a, dist, axis=1), 1.0)
        b_prev = jnp.where(keep, pltpu.roll(b, dist, axis=1), 0.0)
        b = a * b_prev + b
        a = a * a_prev
    return a, b


def _group_scan_const_complex(xr, xi, lam_ref, ns):
    row = lax.broadcasted_iota(jnp.int32, xr.shape, 1)
    for k, dist in enumerate((1, 2, 4)):
        lr = lam_ref[k:k + 1, :ns]
        li = lam_ref[k:k + 1, ns:]
        keep = row >= dist
        pr = jnp.where(keep, pltpu.roll(xr, dist, axis=1), 0.0)
        pi = jnp.where(keep, pltpu.roll(xi, dist, axis=1), 0.0)
        xr, xi = xr + (lr * pr - li * pi), xi + (lr * pi + li * pr)
    return xr, xi


def _rg_body(u_ref, gate_ref, cw_ref, cb_ref, wax_ref, bax_ref, sp_ref, h0_ref, tail0_ref,
             y_ref, ht_ref, tailt_ref, ext_ref, a_ref, b_ref, hc_ref, *, conv_width):
    t = pl.program_id(2)
    tc, width = u_ref.shape
    hpc, hd, _ = wax_ref.shape
    ng = tc // V7X_SUBLANES

    @pl.when(t == 0)
    def _():
        ext_ref[0:8, :] = tail0_ref[...]
        hc_ref[...] = h0_ref[...]

    ext_ref[8:8 + tc, :] = u_ref[...]
    for hh in range(hpc):
        cols = slice(hh * hd, (hh + 1) * hd)
        xc = cb_ref[:, cols] + cw_ref[conv_width - 1:conv_width, cols] * u_ref[:, cols]
        for k in range(conv_width - 1):
            back = conv_width - 1 - k
            xc = xc + cw_ref[k:k + 1, cols] * ext_ref[pl.ds(8 - back, tc), cols]
        pre = jnp.dot(xc.astype(BF16), wax_ref[hh], preferred_element_type=F32) + bax_ref[hh]
        r = _sigmoid(pre[:, :hd])
        i = _sigmoid(pre[:, hd:])
        a = jnp.exp((-RG_C) * r * sp_ref[:, cols])
        b = jnp.sqrt(1.0 - a * a) * i * xc
        a3, b3 = _group_scan_real(a.reshape(ng, 8, hd), b.reshape(ng, 8, hd))
        a_ref[:, cols] = a3.reshape(tc, hd)
        b_ref[:, cols] = b3.reshape(tc, hd)
    ext_ref[0:8, :] = ext_ref[tc:tc + 8, :]
    y_ref[...] = _gelu_tanh(gate_ref[...])

    def body(g, hc):
        r0 = pl.multiple_of(g * 8, 8)
        h = a_ref[pl.ds(r0, 8), :] * hc + b_ref[pl.ds(r0, 8), :]
        y_ref[pl.ds(r0, 8), :] = h * y_ref[pl.ds(r0, 8), :]
        return jnp.broadcast_to(h[7:8, :], (8, width))

    hc = lax.fori_loop(0, ng, body, hc_ref[...], unroll=2)
    hc_ref[...] = hc

    @pl.when(t == pl.num_programs(2) - 1)
    def _():
        ht_ref[...] = hc
        tailt_ref[...] = ext_ref[0:8, :]


def _rg_lru(proj, conv_w, conv_b, wax, bax, softplus_neg_lam, h0, tail0, *, d_rg):
    bsz, t_len, _ = proj.shape
    heads, hd, _ = wax.shape
    conv_width = conv_w.shape[0]
    tc = _divisor_tile(t_len, 512, V7X_SUBLANES)
    hpc = _divisor_tile(heads, max(1, 1024 // hd), 1)
    wd = hpc * hd
    ncell = heads // hpc
    vmem = 2 * 3 * tc * wd * 4 + (tc + 8) * wd * 4 + 2 * tc * wd * 4 + 2 * hpc * hd * 2 * hd * 2 + 8 * tc * hd * 4
    body = functools.partial(_rg_body, conv_width=conv_width)
    hsel = lambda b, h, t: (0, 0, h)
    return pl.pallas_call(
        body,
        grid=(bsz, ncell, t_len // tc),
        in_specs=[
            pl.BlockSpec((None, tc, wd), lambda b, h, t: (b, t, h)),
            pl.BlockSpec((None, tc, wd), lambda b, h, t: (b, t, ncell + h)),
            pl.BlockSpec((conv_width, wd), lambda b, h, t: (0, h)),
            pl.BlockSpec((1, wd), lambda b, h, t: (0, h)),
            pl.BlockSpec((hpc, hd, 2 * hd), lambda b, h, t: (h, 0, 0)),
            pl.BlockSpec((hpc, 1, 2 * hd), lambda b, h, t: (h, 0, 0)),
            pl.BlockSpec((1, wd), lambda b, h, t: (0, h)),
            pl.BlockSpec((None, 8, wd), hsel),
            pl.BlockSpec((None, 8, wd), hsel),
        ],
        out_specs=[
            pl.BlockSpec((None, tc, wd), lambda b, h, t: (b, t, h)),
            pl.BlockSpec((None, 8, wd), lambda b, h, t: (b, 0, h)),
            pl.BlockSpec((None, 8, wd), lambda b, h, t: (b, 0, h)),
        ],
        out_shape=[
            jax.ShapeDtypeStruct((bsz, t_len, d_rg), F32),
            jax.ShapeDtypeStruct((bsz, 8, d_rg), F32),
            jax.ShapeDtypeStruct((bsz, 8, d_rg), F32),
        ],
        scratch_shapes=[
            pltpu.VMEM((tc + 8, wd), F32),
            pltpu.VMEM((tc, wd), F32),
            pltpu.VMEM((tc, wd), F32),
            pltpu.VMEM((8, wd), F32),
        ],
        compiler_params=_params(("parallel", "parallel", "arbitrary"), vmem),
        name="rg_lru",
    )(proj, proj, conv_w, conv_b.reshape(1, d_rg), wax, bax, softplus_neg_lam.reshape(1, d_rg),
      h0, tail0)


def _s5_body(u_ref, km_ref, wsm_ref, vm_ref, e1_ref, e2_ref, mm_ref, mw_ref, mv_ref, lam_ref, p8_ref, d_ref,
             x0_ref, z_ref, xt_ref, mi_ref, ws_ref, v_ref, s_ref, xc_ref):
    t = pl.program_id(2)
    tc, slab = u_ref.shape
    nc = tc // S5_CHUNK
    ns = lam_ref.shape[1] // 2
    ng = nc // V7X_SUBLANES

    @pl.when(jnp.logical_and(pl.program_id(1) == 0, t == 0))
    def _():
        def expand(small_ref, e_ref, mask_ref, dst_ref):
            width = dst_ref.shape[1]
            piece = min(width, 2 * V7X_LANES)
            small = small_ref[...].astype(BF16)
            for q0 in range(0, width, piece):
                full = jnp.dot(small, e_ref[:, q0:q0 + piece], preferred_element_type=F32)
                dst_ref[:, q0:q0 + piece] = (full * mask_ref[:, q0:q0 + piece].astype(F32)).astype(BF16)

        expand(km_ref, e1_ref, mm_ref, mi_ref)
        expand(wsm_ref, e2_ref, mw_ref, ws_ref)
        expand(vm_ref, e1_ref, mv_ref, v_ref)

    @pl.when(t == 0)
    def _():
        xc_ref[...] = x0_ref[...]

    us = [u_ref[pl.ds(i, nc, stride=S5_CHUNK), :] for i in range(S5_CHUNK)]
    ucat = jnp.concatenate([ui.astype(BF16) for ui in us], axis=-1)
    send = jnp.dot(ucat, ws_ref[...], preferred_element_type=F32)
    ycat = jnp.dot(ucat, mi_ref[...], preferred_element_type=F32)

    xr, xi = _group_scan_const_complex(send[:, :ns].reshape(ng, 8, ns), send[:, ns:].reshape(ng, 8, ns),
                                       lam_ref, ns)
    s_ref[7:8, :] = xc_ref[0:1, :]
    s_ref[8:8 + nc, :ns] = xr.reshape(nc, ns)
    s_ref[8:8 + nc, ns:] = xi.reshape(nc, ns)

    p8r = p8_ref[:, :ns]
    p8i = p8_ref[:, ns:]

    def body(g, carry):
        cr, ci = carry
        r0 = pl.multiple_of(8 + g * 8, 8)
        nr = s_ref[pl.ds(r0, 8), :ns] + (p8r * cr - p8i * ci)
        ni = s_ref[pl.ds(r0, 8), ns:] + (p8r * ci + p8i * cr)
        s_ref[pl.ds(r0, 8), :ns] = nr
        s_ref[pl.ds(r0, 8), ns:] = ni
        return (jnp.broadcast_to(nr[7:8, :], (8, ns)), jnp.broadcast_to(ni[7:8, :], (8, ns)))

    cr, ci = lax.fori_loop(0, ng, body, (xc_ref[:, :ns], xc_ref[:, ns:]), unroll=True)
    xc_ref[:, :ns] = cr
    xc_ref[:, ns:] = ci

    x_in = s_ref[pl.ds(7, nc), :]
    ycat = ycat + jnp.dot(x_in.astype(BF16), v_ref[...], preferred_element_type=F32)
    for i in range(S5_CHUNK):
        y = ycat[:, i * slab:(i + 1) * slab] + d_ref[...] * us[i]
        z_ref[pl.ds(i, nc, stride=S5_CHUNK), :] = _gelu_tanh(y)

    @pl.when(t == pl.num_programs(2) - 1)
    def _():
        xt_ref[...] = xc_ref[...]


def _s5_scan(proj, tables, consts, d_skip, x0, *, col0, d_s5):
    bsz, t_len, _ = proj.shape
    k_m, w_m, v_m, lam_pow, p8 = tables
    e1, e2, mask_m, mask_w, mask_v = consts
    nslab, kdim, _ = k_m.shape
    slab = kdim // S5_CHUNK
    ns2 = lam_pow.shape[-1]
    tc = _divisor_tile(t_len, 4096, S5_CHUNK * V7X_SUBLANES)
    assert tc % (S5_CHUNK * V7X_SUBLANES) == 0, (t_len, tc)
    nc = tc // S5_CHUNK
    cb0 = col0 // slab
    big = 2 * kdim * kdim + kdim * ns2
    vmem = (4 * tc * slab * 4 + 2 * big * 2 + big * 2 + 2 * (2 * kdim + ns2) * slab * 4 + (nc + 8) * ns2 * 4
            + nc * (2 * kdim * 4 + kdim * 2 + 3 * ns2 * 4))
    slab_blk = lambda a: pl.BlockSpec((None,) + a.shape[1:], lambda s, b, t: (s, 0, 0))
    const_blk = lambda a: pl.BlockSpec(a.shape, lambda s, b, t: (0, 0))
    return pl.pallas_call(
        _s5_body,
        grid=(nslab, bsz, t_len // tc),
        in_specs=[
            pl.BlockSpec((None, tc, slab), lambda s, b, t: (b, t, cb0 + s)),
            slab_blk(k_m), slab_blk(w_m), slab_blk(v_m),
            const_blk(e1), const_blk(e2), const_blk(mask_m), const_blk(mask_w), const_blk(mask_v),
            pl.BlockSpec((None, 8, ns2), lambda s, b, t: (s, 0, 0)),
            pl.BlockSpec((None, 8, ns2), lambda s, b, t: (s, 0, 0)),
            pl.BlockSpec((1, slab), lambda s, b, t: (0, s)),
            pl.BlockSpec((None, None, 8, ns2), lambda s, b, t: (0, s, 0, 0)),
        ],
        out_specs=[
            pl.BlockSpec((None, tc, slab), lambda s, b, t: (b, t, s)),
            pl.BlockSpec((None, None, 8, ns2), lambda s, b, t: (b, s, 0, 0)),
        ],
        out_shape=[
            jax.ShapeDtypeStruct((bsz, t_len, d_s5), F32),
            jax.ShapeDtypeStruct((bsz, nslab, 8, ns2), F32),
        ],
        scratch_shapes=[
            pltpu.VMEM((kdim, kdim), BF16),
            pltpu.VMEM((kdim, ns2), BF16),
            pltpu.VMEM((ns2, kdim), BF16),
            pltpu.VMEM((nc + 8, ns2), F32),
            pltpu.VMEM((8, ns2), F32),
        ],
        compiler_params=_params(("parallel", "arbitrary", "arbitrary"), vmem),
        name="s5_scan",
    )(proj, k_m, w_m, v_m, e1, e2, mask_m, mask_w, mask_v, lam_pow, p8, d_skip.reshape(1, d_s5), x0)


def _glu_body(z_ref, zc_ref, w_ref, b_ref, o_ref, zb_ref):
    tm = z_ref.shape[0]
    rc = _row_chunk(tm)

    @pl.when(pl.program_id(1) == 0)
    def _():
        def body(c, _):
            r0 = pl.multiple_of(c * rc, rc)
            zb_ref[pl.ds(r0, rc), :] = z_ref[pl.ds(r0, rc), :].astype(BF16)
            return 0

        lax.fori_loop(0, tm // rc, body, 0, unroll=2)

    pre = jnp.dot(zb_ref[...], w_ref[...], preferred_element_type=F32) + b_ref[...]
    o_ref[...] = zc_ref[...] * _sigmoid(pre)


def _glu(z, w, b):
    m, d = z.shape
    tm = _divisor_tile(m, 1024, 16)
    tn = _divisor_tile(d, 512, V7X_LANES)
    vmem = 2 * tm * d * 4 + tm * d * 2 + 2 * d * tn * 2 + 4 * tm * tn * 4
    return pl.pallas_call(
        _glu_body,
        grid=(m // tm, d // tn),
        in_specs=[
            pl.BlockSpec((tm, d), lambda i, j: (i, 0)),
            pl.BlockSpec((tm, tn), lambda i, j: (i, j)),
            pl.BlockSpec((d, tn), lambda i, j: (0, j)),
            pl.BlockSpec((1, tn), lambda i, j: (0, j)),
        ],
        out_specs=pl.BlockSpec((tm, tn), lambda i, j: (i, j)),
        out_shape=jax.ShapeDtypeStruct((m, d), F32),
        scratch_shapes=[pltpu.VMEM((tm, d), BF16)],
        compiler_params=_params(("parallel", "arbitrary"), vmem),
        name="s5_glu",
    )(z, z, w, b.reshape(1, d))


def _outproj_body(yr_ref, ys_ref, gr_ref, gs_ref, w_ref, h_ref, o_ref, yn_ref):
    tm, d_rg = yr_ref.shape
    rc = _row_chunk(tm)

    @pl.when(pl.program_id(1) == 0)
    def _():
        def body(c, _):
            r0 = pl.multiple_of(c * rc, rc)
            yn_ref[pl.ds(r0, rc), :d_rg] = _rms_rows(yr_ref[pl.ds(r0, rc), :], gr_ref[...]).astype(BF16)
            yn_ref[pl.ds(r0, rc), d_rg:] = _rms_rows(ys_ref[pl.ds(r0, rc), :], gs_ref[...]).astype(BF16)
            return 0

        lax.fori_loop(0, tm // rc, body, 0, unroll=2)

    o_ref[...] = h_ref[...] + jnp.dot(yn_ref[...], w_ref[...], preferred_element_type=F32)


def _outproj(y_rg, y_s5, g_rg, g_s5, w, h):
    m, d_rg = y_rg.shape
    d_s5 = y_s5.shape[1]
    d_mix, d = w.shape
    tm = _divisor_tile(m, 512, 16)
    tn = _divisor_tile(d, 1024, V7X_LANES)
    vmem = 2 * tm * d_mix * 4 + tm * d_mix * 2 + 2 * d_mix * tn * 2 + 4 * tm * tn * 4
    return pl.pallas_call(
        _outproj_body,
        grid=(m // tm, d // tn),
        in_specs=[
            pl.BlockSpec((tm, d_rg), lambda i, j: (i, 0)),
            pl.BlockSpec((tm, d_s5), lambda i, j: (i, 0)),
            pl.BlockSpec((1, d_rg), lambda i, j: (0, 0)),
            pl.BlockSpec((1, d_s5), lambda i, j: (0, 0)),
            pl.BlockSpec((d_mix, tn), lambda i, j: (0, j)),
            pl.BlockSpec((tm, tn), lambda i, j: (i, j)),
        ],
        out_specs=pl.BlockSpec((tm, tn), lambda i, j: (i, j)),
        out_shape=jax.ShapeDtypeStruct((m, d), F32),
        scratch_shapes=[pltpu.VMEM((tm, d_mix), BF16)],
        compiler_params=_params(("parallel", "arbitrary"), vmem),
        name="out_proj",
    )(y_rg, y_s5, g_rg.reshape(1, d_rg), g_s5.reshape(1, d_s5), w, h)


def _s5_tables(lam_re, lam_im, log_dt, b_re, b_im, c_re, c_im, slab):
    g, n = lam_re.shape
    c = b_re.shape[-1]
    gs = slab // c
    nslab = g // gs
    ell = S5_CHUNK
    dt = jnp.exp(log_dt.astype(F32))[:, None]
    lam = lax.complex(lam_re.astype(F32), lam_im.astype(F32))
    lam_dt = lam * dt
    lam_bar = jnp.exp(lam_dt)
    b_bar = ((lam_bar - 1.0) / lam)[..., None] * lax.complex(b_re.astype(F32), b_im.astype(F32))
    cc = lax.complex(c_re.astype(F32), c_im.astype(F32))

    def lam_pow(ks):
        return jnp.exp(lam_dt[None] * jnp.asarray(ks, F32)[:, None, None])

    def state_lanes(p):
        p = p.reshape(p.shape[0], nslab, gs * n)
        return jnp.concatenate([jnp.real(p), jnp.imag(p)], axis=-1).transpose(1, 0, 2)

    lam_tab = state_lanes(lam_pow([ell, 2 * ell, 4 * ell] + [ell] * 5))
    p8 = state_lanes(lam_pow([ell * (r + 1) for r in range(8)]))

    pw = lam_pow(list(range(ell + 1)))
    kk = jnp.real(jnp.einsum('gdn,kgn,gnc->kgdc', cc, pw[:ell], b_bar))
    lag = jnp.arange(ell)[None, :] - jnp.arange(ell)[:, None]
    kt = jnp.where((lag >= 0)[:, :, None, None, None], kk[jnp.clip(lag, 0, ell - 1)], 0.0)
    k_m = kt.reshape(ell, ell, nslab, gs, c, c).transpose(2, 0, 3, 5, 1, 4).reshape(nslab, ell * gs * c, ell * c)
    wst = pw[ell - 1 - jnp.arange(ell)][..., None] * b_bar[None]
    wst = wst.reshape(ell, nslab, gs, n, c).transpose(1, 0, 2, 4, 3).reshape(nslab, ell * gs * c, n)
    w_m = jnp.concatenate([jnp.real(wst), jnp.imag(wst)], axis=-1)
    gg = cc[None] * pw[1:ell + 1][:, :, None, :]
    gg = gg.reshape(ell, nslab, gs, c, n).transpose(1, 2, 4, 0, 3).reshape(nslab, gs * n, ell * c)
    v_m = jnp.concatenate([jnp.real(gg), -jnp.imag(gg)], axis=1)
    return k_m, w_m, v_m, lam_tab, p8


def _s5_spread_consts(gs, c, n):
    ell = S5_CHUNK
    a1, q1 = np.arange(ell * c)[:, None], np.arange(ell * gs * c)[None, :]
    e1 = (a1 // c == q1 // (gs * c)) & (a1 % c == q1 % c)
    a2, q2 = np.arange(2 * n)[:, None], np.arange(2 * gs * n)[None, :]
    e2 = (a2 // n == q2 // (gs * n)) & (a2 % n == q2 % n)
    g_in = (np.arange(ell * gs * c) // c) % gs
    g_st = (np.arange(2 * gs * n) % (gs * n)) // n
    mask_m = g_in[:, None] == g_in[None, :]
    mask_w = g_in[:, None] == g_st[None, :]
    mask_v = g_st[:, None] == g_in[None, :]
    return tuple(jnp.asarray(m, dtype=BF16) for m in (e1, e2, mask_m, mask_w, mask_v))


def kernel(x, meta_tokens, ffn1_norm, ffn1_w_gate, ffn1_w_up, ffn1_w_down, mix_norm, w_in,
           rg_conv_w, rg_conv_b, rg_w_a, rg_b_a, rg_w_x, rg_b_x, rg_lambda,
           s5_lambda_re, s5_lambda_im, s5_log_dt, s5_b_re, s5_b_im, s5_c_re, s5_c_im, s5_d,
           s5_glu_w, s5_glu_b, rg_out_norm, s5_out_norm, w_out,
           ffn2_norm, ffn2_w_gate, ffn2_w_up, ffn2_w_down, final_norm):
    bsz, t_len, d = x.shape
    depth = ffn1_norm.shape[0]
    d_rg = rg_lambda.shape[-1]
    d_s5 = s5_d.shape[-1]
    heads, hd = rg_w_a.shape[1], rg_w_a.shape[2]
    slab = V7X_LANES
    s5_align = S5_CHUNK * V7X_SUBLANES

    h = x.reshape(bsz * t_len, d)
    hm = meta_tokens.astype(x.dtype)
    n_meta = hm.shape[0]
    for l in range(depth):
        last = l == depth - 1
        ffn1_w = (ffn1_w_gate[l].astype(BF16), ffn1_w_up[l].astype(BF16), ffn1_w_down[l].astype(BF16))
        ffn2_w = (ffn2_w_gate[l].astype(BF16), ffn2_w_up[l].astype(BF16), ffn2_w_down[l].astype(BF16))
        w_in_b = w_in[l].astype(BF16)
        w_out_b = w_out[l].astype(BF16)
        glu_w_b = s5_glu_w[l].astype(BF16)
        wax = jnp.concatenate([rg_w_a[l], rg_w_x[l]], axis=-1).astype(BF16)
        bax = jnp.concatenate([rg_b_a[l], rg_b_x[l]], axis=-1).reshape(heads, 1, 2 * hd)
        sp = jax.nn.softplus(-rg_lambda[l].astype(F32))
        s5_mats = _s5_tables(s5_lambda_re[l], s5_lambda_im[l], s5_log_dt[l], s5_b_re[l], s5_b_im[l],
                             s5_c_re[l], s5_c_im[l], slab)
        nslab, _, ns2 = s5_mats[3].shape
        s5_consts = _s5_spread_consts(slab // s5_b_re.shape[-1], s5_b_re.shape[-1], s5_lambda_re.shape[-1])
        fnorm = final_norm if last else ffn2_norm[l]

        def mixers(hrows, nb, h0, tail0, x0, front_pad):
            h1 = _ffn(hrows, ffn1_norm[l], *ffn1_w, ffn1_norm[l], final_norm=False)
            proj = _inproj(h1, mix_norm[l], w_in_b).reshape(nb, hrows.shape[0] // nb, -1)
            y_rg, h_t, tail_t = _rg_lru(proj, rg_conv_w[l], rg_conv_b[l], wax, bax, sp, h0, tail0, d_rg=d_rg)
            proj_s5 = jnp.pad(proj, ((0, 0), (front_pad, 0), (0, 0))) if front_pad else proj
            z, x_t = _s5_scan(proj_s5, s5_mats, s5_consts, s5_d[l], x0, col0=2 * d_rg, d_s5=d_s5)
            return h1, y_rg, z[:, front_pad:], h_t, tail_t, x_t

        def tail_layers(h1, y_rg, z):
            m = h1.shape[0]
            y_s5 = _glu(z.reshape(m, d_s5), glu_w_b, s5_glu_b[l])
            h2 = _outproj(y_rg.reshape(m, d_rg), y_s5, rg_out_norm[l], s5_out_norm[l], w_out_b, h1)
            return _ffn(h2, ffn2_norm[l], *ffn2_w, fnorm, final_norm=last)

        assert t_len % s5_align == 0, t_len
        zeros_rg = jnp.zeros((1, 8, d_rg), F32)
        zeros_s5 = jnp.zeros((1, nslab, 8, ns2), F32)
        meta_pad = (-n_meta) % s5_align
        h1m, y_rgm, zm, h_t, tail_t, x_t = mixers(hm, 1, zeros_rg, zeros_rg, zeros_s5, meta_pad)
        h1, y_rg, z, _, _, _ = mixers(h, bsz, h_t, tail_t, x_t, 0)
        h = tail_layers(h1, y_rg, z)
        if not last:
            hm = tail_layers(h1m, y_rgm, zm)
    return h.reshape(bsz, t_len, d)
```

```python
import functools
import math

import jax
import jax.numpy as jnp
import numpy as np
from jax import lax
from jax.experimental import pallas as pl
from jax.experimental.pallas import tpu as pltpu

EPS = 1e-6
RG_C = 8.0
F32 = jnp.float32
BF16 = jnp.bfloat16

V7X_LANES = 128
V7X_SUBLANES = 8
V7X_VMEM_BYTES = 64 * 1024 * 1024

S5_CHUNK = 8


def _divisor_tile(n, pref, align):
    if n <= pref:
        return n
    t = (pref // align) * align
    while t >= align:
        if n % t == 0:
            return t
        t -= align
    return n


def _params(sem, vmem_bytes):
    limit = int(min(V7X_VMEM_BYTES - (2 << 20), max(vmem_bytes + (6 << 20), 32 << 20)))
    return pltpu.CompilerParams(dimension_semantics=sem, vmem_limit_bytes=limit)


def _rms_rows(x, g):
    ms = jnp.mean(x * x, axis=-1, keepdims=True)
    return x * lax.rsqrt(ms + EPS) * g


def _sigmoid(x):
    return 1.0 / (1.0 + jnp.exp(-x))


def _gelu_tanh(x):
    c = math.sqrt(2.0 / math.pi)
    return 0.5 * x * (1.0 + jnp.tanh(c * (x + 0.044715 * (x * x * x))))


def _row_chunk(tm):
    return _divisor_tile(tm, 32, V7X_SUBLANES)


def _ffn_body(x_hbm, g_ref, wg_ref, wu_ref, wd_ref, g2_ref, o_ref, hn_ref, sem, *, n_split, final_norm, n_chunks):
    i = pl.program_id(0)
    j = pl.program_id(1)
    tm, d = o_ref.shape
    ck = tm // n_chunks
    rc = _row_chunk(ck)

    def x_copy(c):
        rows = pl.ds(pl.multiple_of(i * tm + c * ck, ck), ck)
        return pltpu.make_async_copy(x_hbm.at[rows, :], o_ref.at[pl.ds(c * ck, ck), :], sem.at[c])

    @pl.when(j == 0)
    def _():
        for c in range(n_chunks):
            x_copy(c).start()
        for c in range(n_chunks):
            x_copy(c).wait()

            def body(r, _):
                r0 = pl.multiple_of(c * ck + r * rc, rc)
                x = o_ref[pl.ds(r0, rc), :]
                hn_ref[pl.ds(r0, rc), :] = _rms_rows(x, g_ref[...]).astype(BF16)
                o_ref[pl.ds(r0, rc), :] = x + x
                return 0

            lax.fori_loop(0, ck // rc, body, 0, unroll=2)

    g = jnp.dot(hn_ref[...], wg_ref[...], preferred_element_type=F32)
    u = jnp.dot(hn_ref[...], wu_ref[...], preferred_element_type=F32)
    a = (g * _sigmoid(g) * u).astype(BF16)
    dn = d // n_split
    for s in range(n_split):
        o_ref[:, s * dn:(s + 1) * dn] += jnp.dot(a, wd_ref[:, s * dn:(s + 1) * dn],
                                                 preferred_element_type=F32)

    @pl.when(j == pl.num_programs(1) - 1)
    def _():
        def body(c, _):
            r0 = pl.multiple_of(c * rc, rc)
            h = 0.5 * o_ref[pl.ds(r0, rc), :]
            if final_norm:
                h = _rms_rows(h, g2_ref[...])
            o_ref[pl.ds(r0, rc), :] = h
            return 0

        lax.fori_loop(0, tm // rc, body, 0, unroll=2)


def _ffn(x, norm_w, wg, wu, wd, norm2_w, *, final_norm):
    m, d = x.shape
    f = wg.shape[1]
    tf = _divisor_tile(f, 256, V7X_LANES)
    tm = _divisor_tile(m, 1024, 16)
    n_chunks = max(1, tm // 128)
    n_split = max(1, d // 1024)
    vmem = 2 * tm * d * 4 + tm * d * 2 + 6 * (d * tf * 2) + tm * tf * 4 * 4
    body = functools.partial(_ffn_body, n_split=n_split, final_norm=final_norm, n_chunks=n_chunks)
    return pl.pallas_call(
        body,
        grid=(m // tm, f // tf),
        in_specs=[
            pl.BlockSpec(memory_space=pl.ANY),
            pl.BlockSpec((1, d), lambda i, j: (0, 0)),
            pl.BlockSpec((d, tf), lambda i, j: (0, j)),
            pl.BlockSpec((d, tf), lambda i, j: (0, j)),
            pl.BlockSpec((tf, d), lambda i, j: (j, 0)),
            pl.BlockSpec((1, d), lambda i, j: (0, 0)),
        ],
        out_specs=pl.BlockSpec((tm, d), lambda i, j: (i, 0)),
        out_shape=jax.ShapeDtypeStruct((m, d), F32),
        scratch_shapes=[pltpu.VMEM((tm, d), BF16), pltpu.SemaphoreType.DMA((n_chunks,))],
        compiler_params=_params(("parallel", "arbitrary"), vmem),
        name="ffn_swiglu",
    )(x, norm_w.reshape(1, d), wg, wu, wd, norm2_w.reshape(1, d))


def _inproj_body(x_ref, g_ref, w_ref, o_ref, hn_ref):
    tm = x_ref.shape[0]
    rc = _row_chunk(tm)

    @pl.when(pl.program_id(1) == 0)
    def _():
        def body(c, _):
            r0 = pl.multiple_of(c * rc, rc)
            hn_ref[pl.ds(r0, rc), :] = _rms_rows(x_ref[pl.ds(r0, rc), :], g_ref[...]).astype(BF16)
            return 0

        lax.fori_loop(0, tm // rc, body, 0, unroll=2)

    o_ref[...] = jnp.dot(hn_ref[...], w_ref[...], preferred_element_type=F32)


def _inproj(x, norm_w, w):
    m, d = x.shape
    n = w.shape[1]
    tm = _divisor_tile(m, 512, 16)
    tn = _divisor_tile(n, 1024, V7X_LANES)
    vmem = 2 * tm * d * 4 + tm * d * 2 + 2 * d * tn * 2 + 2 * tm * tn * 4
    return pl.pallas_call(
        _inproj_body,
        grid=(m // tm, n // tn),
        in_specs=[
            pl.BlockSpec((tm, d), lambda i, j: (i, 0)),
            pl.BlockSpec((1, d), lambda i, j: (0, 0)),
            pl.BlockSpec((d, tn), lambda i, j: (0, j)),
        ],
        out_specs=pl.BlockSpec((tm, tn), lambda i, j: (i, j)),
        out_shape=jax.ShapeDtypeStruct((m, n), F32),
        scratch_shapes=[pltpu.VMEM((tm, d), BF16)],
        compiler_params=_params(("parallel", "arbitrary"), vmem),
        name="in_proj",
    )(x, norm_w.reshape(1, d), w)


def _group_scan_real(a, b):
    row = lax.broadcasted_iota(jnp.int32, a.shape, 1)
    for dist in (1, 2, 4):
        keep = row >= dist
        a_prev = jnp.where(keep, pltpu.roll(a, dist, axis=1), 1.0)
        b_prev = jnp.where(keep, pltpu.roll(b, dist, axis=1), 0.0)
        b = a * b_prev + b
        a = a * a_prev
    return a, b


def _group_scan_const_complex(xr, xi, lam_ref, ns):
    row = lax.broadcasted_iota(jnp.int32, xr.shape, 1)
    for k, dist in enumerate((1, 2, 4)):
        lr = lam_ref[k:k + 1, :ns]
        li = lam_ref[k:k + 1, ns:]
        keep = row >= dist
        pr = jnp.where(keep, pltpu.roll(xr, dist, axis=1), 0.0)
        pi = jnp.where(keep, pltpu.roll(xi, dist, axis=1), 0.0)
        xr, xi = xr + (lr * pr - li * pi), xi + (lr * pi + li * pr)
    return xr, xi


def _rg_body(u_ref, gate_ref, cw_ref, cb_ref, wax_ref, bax_ref, sp_ref, h0_ref, tail0_ref,
             y_ref, ht_ref, tailt_ref, ext_ref, a_ref, b_ref, hc_ref, *, conv_width):
    t = pl.program_id(2)
    tc, width = u_ref.shape
    hpc, hd, _ = wax_ref.shape
    ng = tc // V7X_SUBLANES

    @pl.when(t == 0)
    def _():
        ext_ref[0:8, :] = tail0_ref[...]
        hc_ref[...] = h0_ref[...]

    ext_ref[8:8 + tc, :] = u_ref[...]
    for hh in range(hpc):
        cols = slice(hh * hd, (hh + 1) * hd)
        xc = cb_ref[:, cols] + cw_ref[conv_width - 1:conv_width, cols] * u_ref[:, cols]
        for k in range(conv_width - 1):
            back = conv_width - 1 - k
            xc = xc + cw_ref[k:k + 1, cols] * ext_ref[pl.ds(8 - back, tc), cols]
        pre = jnp.dot(xc.astype(BF16), wax_ref[hh], preferred_element_type=F32) + bax_ref[hh]
        r = _sigmoid(pre[:, :hd])
        i = _sigmoid(pre[:, hd:])
        a = jnp.exp((-RG_C) * r * sp_ref[:, cols])
        b = jnp.sqrt(1.0 - a * a) * i * xc
        a3, b3 = _group_scan_real(a.reshape(ng, 8, hd), b.reshape(ng, 8, hd))
        a_ref[:, cols] = a3.reshape(tc, hd)
        b_ref[:, cols] = b3.reshape(tc, hd)
    ext_ref[0:8, :] = ext_ref[tc:tc + 8, :]
    y_ref[...] = _gelu_tanh(gate_ref[...])

    def body(g, hc):
        r0 = pl.multiple_of(g * 8, 8)
        h = a_ref[pl.ds(r0, 8), :] * hc + b_ref[pl.ds(r0, 8), :]
        y_ref[pl.ds(r0, 8), :] = h * y_ref[pl.ds(r0, 8), :]
        return jnp.broadcast_to(h[7:8, :], (8, width))

    hc = lax.fori_loop(0, ng, body, hc_ref[...], unroll=2)
    hc_ref[...] = hc

    @pl.when(t == pl.num_programs(2) - 1)
    def _():
        ht_ref[...] = hc
        tailt_ref[...] = ext_ref[0:8, :]


def _rg_lru(proj, conv_w, conv_b, wax, bax, softplus_neg_lam, h0, tail0, *, d_rg):
    bsz, t_len, _ = proj.shape
    heads, hd, _ = wax.shape
    conv_width = conv_w.shape[0]
    tc = _divisor_tile(t_len, 512, V7X_SUBLANES)
    hpc = _divisor_tile(heads, max(1, 1024 // hd), 1)
    wd = hpc * hd
    ncell = heads // hpc
    vmem = 2 * 3 * tc * wd * 4 + (tc + 8) * wd * 4 + 2 * tc * wd * 4 + 2 * hpc * hd * 2 * hd * 2 + 8 * tc * hd * 4
    body = functools.partial(_rg_body, conv_width=conv_width)
    hsel = lambda b, h, t: (0, 0, h)
    return pl.pallas_call(
        body,
        grid=(bsz, ncell, t_len // tc),
        in_specs=[
            pl.BlockSpec((None, tc, wd), lambda b, h, t: (b, t, h)),
            pl.BlockSpec((None, tc, wd), lambda b, h, t: (b, t, ncell + h)),
            pl.BlockSpec((conv_width, wd), lambda b, h, t: (0, h)),
            pl.BlockSpec((1, wd), lambda b, h, t: (0, h)),
            pl.BlockSpec((hpc, hd, 2 * hd), lambda b, h, t: (h, 0, 0)),
            pl.BlockSpec((hpc, 1, 2 * hd), lambda b, h, t: (h, 0, 0)),
            pl.BlockSpec((1, wd), lambda b, h, t: (0, h)),
            pl.BlockSpec((None, 8, wd), hsel),
            pl.BlockSpec((None, 8, wd), hsel),
        ],
        out_specs=[
            pl.BlockSpec((None, tc, wd), lambda b, h, t: (b, t, h)),
            pl.BlockSpec((None, 8, wd), lambda b, h, t: (b, 0, h)),
            pl.BlockSpec((None, 8, wd), lambda b, h, t: (b, 0, h)),
        ],
        out_shape=[
            jax.ShapeDtypeStruct((bsz, t_len, d_rg), F32),
            jax.ShapeDtypeStruct((bsz, 8, d_rg), F32),
            jax.ShapeDtypeStruct((bsz, 8, d_rg), F32),
        ],
        scratch_shapes=[
            pltpu.VMEM((tc + 8, wd), F32),
            pltpu.VMEM((tc, wd), F32),
            pltpu.VMEM((tc, wd), F32),
            pltpu.VMEM((8, wd), F32),
        ],
        compiler_params=_params(("parallel", "parallel", "arbitrary"), vmem),
        name="rg_lru",
    )(proj, proj, conv_w, conv_b.reshape(1, d_rg), wax, bax, softplus_neg_lam.reshape(1, d_rg),
      h0, tail0)


def _s5_body(u_ref, km_ref, wsm_ref, vm_ref, e1_ref, e2_ref, mm_ref, mw_ref, mv_ref, lam_ref, p8_ref, d_ref,
             x0_ref, z_ref, xt_ref, mi_ref, ws_ref, v_ref, s_ref, xc_ref):
    t = pl.program_id(2)
    tc, slab = u_ref.shape
    nc = tc // S5_CHUNK
    ns = lam_ref.shape[1] // 2
    ng = nc // V7X_SUBLANES

    @pl.when(jnp.logical_and(pl.program_id(1) == 0, t == 0))
    def _():
        def expand(small_ref, e_ref, mask_ref, dst_ref):
            width = dst_ref.shape[1]
            piece = min(width, 2 * V7X_LANES)
            small = small_ref[...].astype(BF16)
            for q0 in range(0, width, piece):
                full = jnp.dot(small, e_ref[:, q0:q0 + piece], preferred_element_type=F32)
                dst_ref[:, q0:q0 + piece] = (full * mask_ref[:, q0:q0 + piece].astype(F32)).astype(BF16)

        expand(km_ref, e1_ref, mm_ref, mi_ref)
        expand(wsm_ref, e2_ref, mw_ref, ws_ref)
        expand(vm_ref, e1_ref, mv_ref, v_ref)

    @pl.when(t == 0)
    def _():
        xc_ref[...] = x0_ref[...]

    us = [u_ref[pl.ds(i, nc, stride=S5_CHUNK), :] for i in range(S5_CHUNK)]
    ucat = jnp.concatenate([ui.astype(BF16) for ui in us], axis=-1)
    send = jnp.dot(ucat, ws_ref[...], preferred_element_type=F32)
    ycat = jnp.dot(ucat, mi_ref[...], preferred_element_type=F32)

    xr, xi = _group_scan_const_complex(send[:, :ns].reshape(ng, 8, ns), send[:, ns:].reshape(ng, 8, ns),
                                       lam_ref, ns)
    s_ref[7:8, :] = xc_ref[0:1, :]
    s_ref[8:8 + nc, :ns] = xr.reshape(nc, ns)
    s_ref[8:8 + nc, ns:] = xi.reshape(nc, ns)

    p8r = p8_ref[:, :ns]
    p8i = p8_ref[:, ns:]

    def body(g, carry):
        cr, ci = carry
        r0 = pl.multiple_of(8 + g * 8, 8)
        nr = s_ref[pl.ds(r0, 8), :ns] + (p8r * cr - p8i * ci)
        ni = s_ref[pl.ds(r0, 8), ns:] + (p8r * ci + p8i * cr)
        s_ref[pl.ds(r0, 8), :ns] = nr
        s_ref[pl.ds(r0, 8), ns:] = ni
        return (jnp.broadcast_to(nr[7:8, :], (8, ns)), jnp.broadcast_to(ni[7:8, :], (8, ns)))

    cr, ci = lax.fori_loop(0, ng, body, (xc_ref[:, :ns], xc_ref[:, ns:]), unroll=True)
    xc_ref[:, :ns] = cr
    xc_ref[:, ns:] = ci

    x_in = s_ref[pl.ds(7, nc), :]
    ycat = ycat + jnp.dot(x_in.astype(BF16), v_ref[...], preferred_element_type=F32)
    for i in range(S5_CHUNK):
        y = ycat[:, i * slab:(i + 1) * slab] + d_ref[...] * us[i]
        z_ref[pl.ds(i, nc, stride=S5_CHUNK), :] = _gelu_tanh(y)

    @pl.when(t == pl.num_programs(2) - 1)
    def _():
        xt_ref[...] = xc_ref[...]


def _s5_scan(proj, tables, consts, d_skip, x0, *, col0, d_s5):
    bsz, t_len, _ = proj.shape
    k_m, w_m, v_m, lam_pow, p8 = tables
    e1, e2, mask_m, mask_w, mask_v = consts
    nslab, kdim, _ = k_m.shape
    slab = kdim // S5_CHUNK
    ns2 = lam_pow.shape[-1]
    tc = _divisor_tile(t_len, 4096, S5_CHUNK * V7X_SUBLANES)
    assert tc % (S5_CHUNK * V7X_SUBLANES) == 0, (t_len, tc)
    nc = tc // S5_CHUNK
    cb0 = col0 // slab
    big = 2 * kdim * kdim + kdim * ns2
    vmem = (4 * tc * slab * 4 + 2 * big * 2 + big * 2 + 2 * (2 * kdim + ns2) * slab * 4 + (nc + 8) * ns2 * 4
            + nc * (2 * kdim * 4 + kdim * 2 + 3 * ns2 * 4))
    slab_blk = lambda a: pl.BlockSpec((None,) + a.shape[1:], lambda s, b, t: (s, 0, 0))
    const_blk = lambda a: pl.BlockSpec(a.shape, lambda s, b, t: (0, 0))
    return pl.pallas_call(
        _s5_body,
        grid=(nslab, bsz, t_len // tc),
        in_specs=[
            pl.BlockSpec((None, tc, slab), lambda s, b, t: (b, t, cb0 + s)),
            slab_blk(k_m), slab_blk(w_m), slab_blk(v_m),
            const_blk(e1), const_blk(e2), const_blk(mask_m), const_blk(mask_w), const_blk(mask_v),
            pl.BlockSpec((None, 8, ns2), lambda s, b, t: (s, 0, 0)),
            pl.BlockSpec((None, 8, ns2), lambda s, b, t: (s, 0, 0)),
            pl.BlockSpec((1, slab), lambda s, b, t: (0, s)),
            pl.BlockSpec((None, None, 8, ns2), lambda s, b, t: (0, s, 0, 0)),
        ],
        out_specs=[
            pl.BlockSpec((None, tc, slab), lambda s, b, t: (b, t, s)),
            pl.BlockSpec((None, None, 8, ns2), lambda s, b, t: (b, s, 0, 0)),
        ],
        out_shape=[
            jax.ShapeDtypeStruct((bsz, t_len, d_s5), F32),
            jax.ShapeDtypeStruct((bsz, nslab, 8, ns2), F32),
        ],
        scratch_shapes=[
            pltpu.VMEM((kdim, kdim), BF16),
            pltpu.VMEM((kdim, ns2), BF16),
            pltpu.VMEM((ns2, kdim), BF16),
            pltpu.VMEM((nc + 8, ns2), F32),
            pltpu.VMEM((8, ns2), F32),
        ],
        compiler_params=_params(("parallel", "arbitrary", "arbitrary"), vmem),
        name="s5_scan",
    )(proj, k_m, w_m, v_m, e1, e2, mask_m, mask_w, mask_v, lam_pow, p8, d_skip.reshape(1, d_s5), x0)


def _glu_body(z_ref, zc_ref, w_ref, b_ref, o_ref, zb_ref):
    tm = z_ref.shape[0]
    rc = _row_chunk(tm)

    @pl.when(pl.program_id(1) == 0)
    def _():
        def body(c, _):
            r0 = pl.multiple_of(c * rc, rc)
            zb_ref[pl.ds(r0, rc), :] = z_ref[pl.ds(r0, rc), :].astype(BF16)
            return 0

        lax.fori_loop(0, tm // rc, body, 0, unroll=2)

    pre = jnp.dot(zb_ref[...], w_ref[...], preferred_element_type=F32) + b_ref[...]
    o_ref[...] = zc_ref[...] * _sigmoid(pre)


def _glu(z, w, b):
    m, d = z.shape
    tm = _divisor_tile(m, 1024, 16)
    tn = _divisor_tile(d, 512, V7X_LANES)
    vmem = 2 * tm * d * 4 + tm * d * 2 + 2 * d * tn * 2 + 4 * tm * tn * 4
    return pl.pallas_call(
        _glu_body,
        grid=(m // tm, d // tn),
        in_specs=[
            pl.BlockSpec((tm, d), lambda i, j: (i, 0)),
            pl.BlockSpec((tm, tn), lambda i, j: (i, j)),
            pl.BlockSpec((d, tn), lambda i, j: (0, j)),
            pl.BlockSpec((1, tn), lambda i, j: (0, j)),
        ],
        out_specs=pl.BlockSpec((tm, tn), lambda i, j: (i, j)),
        out_shape=jax.ShapeDtypeStruct((m, d), F32),
        scratch_shapes=[pltpu.VMEM((tm, d), BF16)],
        compiler_params=_params(("parallel", "arbitrary"), vmem),
        name="s5_glu",
    )(z, z, w, b.reshape(1, d))


def _outproj_body(yr_ref, ys_ref, gr_ref, gs_ref, w_ref, h_ref, o_ref, yn_ref):
    tm, d_rg = yr_ref.shape
    rc = _row_chunk(tm)

    @pl.when(pl.program_id(1) == 0)
    def _():
        def body(c, _):
            r0 = pl.multiple_of(c * rc, rc)
            yn_ref[pl.ds(r0, rc), :d_rg] = _rms_rows(yr_ref[pl.ds(r0, rc), :], gr_ref[...]).astype(BF16)
            yn_ref[pl.ds(r0, rc), d_rg:] = _rms_rows(ys_ref[pl.ds(r0, rc), :], gs_ref[...]).astype(BF16)
            return 0

        lax.fori_loop(0, tm // rc, body, 0, unroll=2)

    o_ref[...] = h_ref[...] + jnp.dot(yn_ref[...], w_ref[...], preferred_element_type=F32)


def _outproj(y_rg, y_s5, g_rg, g_s5, w, h):
    m, d_rg = y_rg.shape
    d_s5 = y_s5.shape[1]
    d_mix, d = w.shape
    tm = _divisor_tile(m, 512, 16)
    tn = _divisor_tile(d, 1024, V7X_LANES)
    vmem = 2 * tm * d_mix * 4 + tm * d_mix * 2 + 2 * d_mix * tn * 2 + 4 * tm * tn * 4
    return pl.pallas_call(
        _outproj_body,
        grid=(m // tm, d // tn),
        in_specs=[
            pl.BlockSpec((tm, d_rg), lambda i, j: (i, 0)),
            pl.BlockSpec((tm, d_s5), lambda i, j: (i, 0)),
            pl.BlockSpec((1, d_rg), lambda i, j: (0, 0)),
            pl.BlockSpec((1, d_s5), lambda i, j: (0, 0)),
            pl.BlockSpec((d_mix, tn), lambda i, j: (0, j)),
            pl.BlockSpec((tm, tn), lambda i, j: (i, j)),
        ],
        out_specs=pl.BlockSpec((tm, tn), lambda i, j: (i, j)),
        out_shape=jax.ShapeDtypeStruct((m, d), F32),
        scratch_shapes=[pltpu.VMEM((tm, d_mix), BF16)],
        compiler_params=_params(("parallel", "arbitrary"), vmem),
        name="out_proj",
    )(y_rg, y_s5, g_rg.reshape(1, d_rg), g_s5.reshape(1, d_s5), w, h)


def _s5_tables(lam_re, lam_im, log_dt, b_re, b_im, c_re, c_im, slab):
    g, n = lam_re.shape
    c = b_re.shape[-1]
    gs = slab // c
    nslab = g // gs
    ell = S5_CHUNK
    dt = jnp.exp(log_dt.astype(F32))[:, None]
    lam = lax.complex(lam_re.astype(F32), lam_im.astype(F32))
    lam_dt = lam * dt
    lam_bar = jnp.exp(lam_dt)
    b_bar = ((lam_bar - 1.0) / lam)[..., None] * lax.complex(b_re.astype(F32), b_im.astype(F32))
    cc = lax.complex(c_re.astype(F32), c_im.astype(F32))

    def lam_pow(ks):
        return jnp.exp(lam_dt[None] * jnp.asarray(ks, F32)[:, None, None])

    def state_lanes(p):
        p = p.reshape(p.shape[0], nslab, gs * n)
        return jnp.concatenate([jnp.real(p), jnp.imag(p)], axis=-1).transpose(1, 0, 2)

    lam_tab = state_lanes(lam_pow([ell, 2 * ell, 4 * ell] + [ell] * 5))
    p8 = state_lanes(lam_pow([ell * (r + 1) for r in range(8)]))

    pw = lam_pow(list(range(ell + 1)))
    kk = jnp.real(jnp.einsum('gdn,kgn,gnc->kgdc', cc, pw[:ell], b_bar))
    lag = jnp.arange(ell)[None, :] - jnp.arange(ell)[:, None]
    kt = jnp.where((lag >= 0)[:, :, None, None, None], kk[jnp.clip(lag, 0, ell - 1)], 0.0)
    k_m = kt.reshape(ell, ell, nslab, gs, c, c).transpose(2, 0, 3, 5, 1, 4).reshape(nslab, ell * gs * c, ell * c)
    wst = pw[ell - 1 - jnp.arange(ell)][..., None] * b_bar[None]
    wst = wst.reshape(ell, nslab, gs, n, c).transpose(1, 0, 2, 4, 3).reshape(nslab, ell * gs * c, n)
    w_m = jnp.concatenate([jnp.real(wst), jnp.imag(wst)], axis=-1)
    gg = cc[None] * pw[1:ell + 1][:, :, None, :]
    gg = gg.reshape(ell, nslab, gs, c, n).transpose(1, 2, 4, 0, 3).reshape(nslab, gs * n, ell * c)
    v_m = jnp.concatenate([jnp.real(gg), -jnp.imag(gg)], axis=1)
    return k_m, w_m, v_m, lam_tab, p8


def _s5_spread_consts(gs, c, n):
    ell = S5_CHUNK
    a1, q1 = np.arange(ell * c)[:, None], np.arange(ell * gs * c)[None, :]
    e1 = (a1 // c == q1 // (gs * c)) & (a1 % c == q1 % c)
    a2, q2 = np.arange(2 * n)[:, None], np.arange(2 * gs * n)[None, :]
    e2 = (a2 // n == q2 // (gs * n)) & (a2 % n == q2 % n)
    g_in = (np.arange(ell * gs * c) // c) % gs
    g_st = (np.arange(2 * gs * n) % (gs * n)) // n
    mask_m = g_in[:, None] == g_in[None, :]
    mask_w = g_in[:, None] == g_st[None, :]
    mask_v = g_st[:, None] == g_in[None, :]
    return tuple(jnp.asarray(m, dtype=BF16) for m in (e1, e2, mask_m, mask_w, mask_v))


def kernel(x, meta_tokens, ffn1_norm, ffn1_w_gate, ffn1_w_up, ffn1_w_down, mix_norm, w_in,
           rg_conv_w, rg_conv_b, rg_w_a, rg_b_a, rg_w_x, rg_b_x, rg_lambda,
           s5_lambda_re, s5_lambda_im, s5_log_dt, s5_b_re, s5_b_im, s5_c_re, s5_c_im, s5_d,
           s5_glu_w, s5_glu_b, rg_out_norm, s5_out_norm, w_out,
           ffn2_norm, ffn2_w_gate, ffn2_w_up, ffn2_w_down, final_norm):
    bsz, t_len, d = x.shape
    depth = ffn1_norm.shape[0]
    d_rg = rg_lambda.shape[-1]
    d_s5 = s5_d.shape[-1]
    heads, hd = rg_w_a.shape[1], rg_w_a.shape[2]
    slab = V7X_LANES
    s5_align = S5_CHUNK * V7X_SUBLANES

    h = x.reshape(bsz * t_len, d)
    hm = meta_tokens.astype(x.dtype)
    n_meta = hm.shape[0]
    for l in range(depth):
        last = l == depth - 1
        ffn1_w = (ffn1_w_gate[l].astype(BF16), ffn1_w_up[l].astype(BF16), ffn1_w_down[l].astype(BF16))
        ffn2_w = (ffn2_w_gate[l].astype(BF16), ffn2_w_up[l].astype(BF16), ffn2_w_down[l].astype(BF16))
        w_in_b = w_in[l].astype(BF16)
        w_out_b = w_out[l].astype(BF16)
        glu_w_b = s5_glu_w[l].astype(BF16)
        wax = jnp.concatenate([rg_w_a[l], rg_w_x[l]], axis=-1).astype(BF16)
        bax = jnp.concatenate([rg_b_a[l], rg_b_x[l]], axis=-1).reshape(heads, 1, 2 * hd)
        sp = jax.nn.softplus(-rg_lambda[l].astype(F32))
        s5_mats = _s5_tables(s5_lambda_re[l], s5_lambda_im[l], s5_log_dt[l], s5_b_re[l], s5_b_im[l],
                             s5_c_re[l], s5_c_im[l], slab)
        nslab, _, ns2 = s5_mats[3].shape
        s5_consts = _s5_spread_consts(slab // s5_b_re.shape[-1], s5_b_re.shape[-1], s5_lambda_re.shape[-1])
        fnorm = final_norm if last else ffn2_norm[l]

        def mixers(hrows, nb, h0, tail0, x0, front_pad):
            h1 = _ffn(hrows, ffn1_norm[l], *ffn1_w, ffn1_norm[l], final_norm=False)
            proj = _inproj(h1, mix_norm[l], w_in_b).reshape(nb, hrows.shape[0] // nb, -1)
            y_rg, h_t, tail_t = _rg_lru(proj, rg_conv_w[l], rg_conv_b[l], wax, bax, sp, h0, tail0, d_rg=d_rg)
            proj_s5 = jnp.pad(proj, ((0, 0), (front_pad, 0), (0, 0))) if front_pad else proj
            z, x_t = _s5_scan(proj_s5, s5_mats, s5_consts, s5_d[l], x0, col0=2 * d_rg, d_s5=d_s5)
            return h1, y_rg, z[:, front_pad:], h_t, tail_t, x_t

        def tail_layers(h1, y_rg, z):
            m = h1.shape[0]
            y_s5 = _glu(z.reshape(m, d_s5), glu_w_b, s5_glu_b[l])
            h2 = _outproj(y_rg.reshape(m, d_rg), y_s5, rg_out_norm[l], s5_out_norm[l], w_out_b, h1)
            return _ffn(h2, ffn2_norm[l], *ffn2_w, fnorm, final_norm=last)

        assert t_len % s5_align == 0, t_len
        zeros_rg = jnp.zeros((1, 8, d_rg), F32)
        zeros_s5 = jnp.zeros((1, nslab, 8, ns2), F32)
        meta_pad = (-n_meta) % s5_align
        h1m, y_rgm, zm, h_t, tail_t, x_t = mixers(hm, 1, zeros_rg, zeros_rg, zeros_s5, meta_pad)
        h1, y_rg, z, _, _, _ = mixers(h, bsz, h_t, tail_t, x_t, 0)
        h = tail_layers(h1, y_rg, z)
        if not last:
            hm = tail_layers(h1m, y_rgm, zm)
    return h.reshape(bsz, t_len, d)
```

```python
import functools
import math

import jax
import jax.numpy as jnp
import numpy as np
from jax import lax
from jax.experimental import pallas as pl
from jax.experimental.pallas import tpu as pltpu

EPS = 1e-6
RG_C = 8.0
F32 = jnp.float32
BF16 = jnp.bfloat16

V7X_LANES = 128
V7X_SUBLANES = 8
V7X_VMEM_BYTES = 64 * 1024 * 1024

S5_CHUNK = 8


def _divisor_tile(n, pref, align):
    if n <= pref:
        return n
    t = (pref // align) * align
    while t >= align:
        if n % t == 0:
            return t
        t -= align
    return n


def _params(sem, vmem_bytes):
    limit = int(min(V7X_VMEM_BYTES - (2 << 20), max(vmem_bytes + (6 << 20), 32 << 20)))
    return pltpu.CompilerParams(dimension_semantics=sem, vmem_limit_bytes=limit)


def _rms_rows(x, g):
    ms = jnp.mean(x * x, axis=-1, keepdims=True)
    return x * lax.rsqrt(ms + EPS) * g


def _sigmoid(x):
    return 1.0 / (1.0 + jnp.exp(-x))


def _gelu_tanh(x):
    c = math.sqrt(2.0 / math.pi)
    return 0.5 * x * (1.0 + jnp.tanh(c * (x + 0.044715 * (x * x * x))))


def _row_chunk(tm):
    return _divisor_tile(tm, 32, V7X_SUBLANES)


def _ffn_body(x_hbm, g_ref, wg_ref, wu_ref, wd_ref, g2_ref, o_ref, hn_ref, sem, *, n_split, final_norm, n_chunks):
    i = pl.program_id(0)
    j = pl.program_id(1)
    tm, d = o_ref.shape
    ck = tm // n_chunks
    rc = _row_chunk(ck)

    def x_copy(c):
        rows = pl.ds(pl.multiple_of(i * tm + c * ck, ck), ck)
        return pltpu.make_async_copy(x_hbm.at[rows, :], o_ref.at[pl.ds(c * ck, ck), :], sem.at[c])

    @pl.when(j == 0)
    def _():
        for c in range(n_chunks):
            x_copy(c).start()
        for c in range(n_chunks):
            x_copy(c).wait()

            def body(r, _):
                r0 = pl.multiple_of(c * ck + r * rc, rc)
                x = o_ref[pl.ds(r0, rc), :]
                hn_ref[pl.ds(r0, rc), :] = _rms_rows(x, g_ref[...]).astype(BF16)
                o_ref[pl.ds(r0, rc), :] = x + x
                return 0

            lax.fori_loop(0, ck // rc, body, 0, unroll=2)

    g = jnp.dot(hn_ref[...], wg_ref[...], preferred_element_type=F32)
    u = jnp.dot(hn_ref[...], wu_ref[...], preferred_element_type=F32)
    a = (g * _sigmoid(g) * u).astype(BF16)
    dn = d // n_split
    for s in range(n_split):
        o_ref[:, s * dn:(s + 1) * dn] += jnp.dot(a, wd_ref[:, s * dn:(s + 1) * dn],
                                                 preferred_element_type=F32)

    @pl.when(j == pl.num_programs(1) - 1)
    def _():
        def body(c, _):
            r0 = pl.multiple_of(c * rc, rc)
            h = 0.5 * o_ref[pl.ds(r0, rc), :]
            if final_norm:
                h = _rms_rows(h, g2_ref[...])
            o_ref[pl.ds(r0, rc), :] = h
            return 0

        lax.fori_loop(0, tm // rc, body, 0, unroll=2)


def _ffn(x, norm_w, wg, wu, wd, norm2_w, *, final_norm):
    m, d = x.shape
    f = wg.shape[1]
    tf = _divisor_tile(f, 256, V7X_LANES)
    tm = _divisor_tile(m, 1024, 16)
    n_chunks = max(1, tm // 128)
    n_split = max(1, d // 1024)
    vmem = 2 * tm * d * 4 + tm * d * 2 + 6 * (d * tf * 2) + tm * tf * 4 * 4
    body = functools.partial(_ffn_body, n_split=n_split, final_norm=final_norm, n_chunks=n_chunks)
    return pl.pallas_call(
        body,
        grid=(m // tm, f // tf),
        in_specs=[
            pl.BlockSpec(memory_space=pl.ANY),
            pl.BlockSpec((1, d), lambda i, j: (0, 0)),
            pl.BlockSpec((d, tf), lambda i, j: (0, j)),
            pl.BlockSpec((d, tf), lambda i, j: (0, j)),
            pl.BlockSpec((tf, d), lambda i, j: (j, 0)),
            pl.BlockSpec((1, d), lambda i, j: (0, 0)),
        ],
        out_specs=pl.BlockSpec((tm, d), lambda i, j: (i, 0)),
        out_shape=jax.ShapeDtypeStruct((m, d), F32),
        scratch_shapes=[pltpu.VMEM((tm, d), BF16), pltpu.SemaphoreType.DMA((n_chunks,))],
        compiler_params=_params(("parallel", "arbitrary"), vmem),
        name="ffn_swiglu",
    )(x, norm_w.reshape(1, d), wg, wu, wd, norm2_w.reshape(1, d))


def _stream_rows(srcs, bufs, sems, row0, ck, n_chunks, consume):
    def copy(k, c):
        rows = pl.ds(pl.multiple_of(row0 + c * ck, ck), ck)
        return pltpu.make_async_copy(srcs[k].at[rows, :], bufs[k].at[c % 2], sems[k].at[c % 2])

    for k in range(len(srcs)):
        copy(k, 0).start()
    for c in range(n_chunks):
        for k in range(len(srcs)):
            if c + 1 < n_chunks:
                copy(k, c + 1).start()
            copy(k, c).wait()
        consume(c, [b.at[c % 2] for b in bufs])


def _chunks(tm):
    ck = _divisor_tile(tm, 128, 16)
    return tm // ck, ck


def _inproj_body(x_hbm, g_ref, w_ref, o_ref, hn_ref, xbuf, sem, *, n_chunks, ck):
    tm = hn_ref.shape[0]
    rc = _row_chunk(ck)

    @pl.when(pl.program_id(1) == 0)
    def _():
        def consume(c, views):
            def body(r, _):
                r0 = pl.multiple_of(r * rc, rc)
                hn_ref[pl.ds(c * ck + r0, rc), :] = _rms_rows(views[0][pl.ds(r0, rc), :], g_ref[...]).astype(BF16)
                return 0

            lax.fori_loop(0, ck // rc, body, 0, unroll=2)

        _stream_rows([x_hbm], [xbuf], [sem], pl.program_id(0) * tm, ck, n_chunks, consume)

    o_ref[...] = jnp.dot(hn_ref[...], w_ref[...], preferred_element_type=F32)


def _inproj(x, norm_w, w):
    m, d = x.shape
    n = w.shape[1]
    tm = _divisor_tile(m, 1024, 16)
    tn = _divisor_tile(n, 1024, V7X_LANES)
    n_chunks, ck = _chunks(tm)
    vmem = 2 * ck * d * 4 + tm * d * 2 + 2 * d * tn * 2 + 3 * tm * tn * 4
    return pl.pallas_call(
        functools.partial(_inproj_body, n_chunks=n_chunks, ck=ck),
        grid=(m // tm, n // tn),
        in_specs=[
            pl.BlockSpec(memory_space=pl.ANY),
            pl.BlockSpec((1, d), lambda i, j: (0, 0)),
            pl.BlockSpec((d, tn), lambda i, j: (0, j)),
        ],
        out_specs=pl.BlockSpec((tm, tn), lambda i, j: (i, j)),
        out_shape=jax.ShapeDtypeStruct((m, n), F32),
        scratch_shapes=[pltpu.VMEM((tm, d), BF16), pltpu.VMEM((2, ck, d), F32), pltpu.SemaphoreType.DMA((2,))],
        compiler_params=_params(("parallel", "arbitrary"), vmem),
        name="in_proj",
    )(x, norm_w.reshape(1, d), w)


def _group_scan_real(a, b):
    row = lax.broadcasted_iota(jnp.int32, a.shape, 1)
    for dist in (1, 2, 4):
        keep = row >= dist
        a_prev = jnp.where(keep, pltpu.roll(a, dist, axis=1), 1.0)
        b_prev = jnp.where(keep, pltpu.roll(b, dist, axis=1), 0.0)
        b = a * b_prev + b
        a = a * a_prev
    return a, b


def _group_scan_const_complex(xr, xi, lam_ref, ns):
    row = lax.broadcasted_iota(jnp.int32, xr.shape, 1)
    for k, dist in enumerate((1, 2, 4)):
        lr = lam_ref[k:k + 1, :ns]
        li = lam_ref[k:k + 1, ns:]
        keep = row >= dist
        pr = jnp.where(keep, pltpu.roll(xr, dist, axis=1), 0.0)
        pi = jnp.where(keep, pltpu.roll(xi, dist, axis=1), 0.0)
        xr, xi = xr + (lr * pr - li * pi), xi + (lr * pi + li * pr)
    return xr, xi


def _rg_body(u_ref, gate_ref, cw_ref, cb_ref, wax_ref, bax_ref, sp_ref, h0_ref, tail0_ref,
             y_ref, ht_ref, tailt_ref, ext_ref, a_ref, b_ref, hc_ref, *, conv_width):
    t = pl.program_id(2)
    tc, width = u_ref.shape
    hpc, hd, _ = wax_ref.shape
    ng = tc // V7X_SUBLANES

    @pl.when(t == 0)
    def _():
        ext_ref[0:8, :] = tail0_ref[...]
        hc_ref[...] = h0_ref[...]

    ext_ref[8:8 + tc, :] = u_ref[...]
    for hh in range(hpc):
        cols = slice(hh * hd, (hh + 1) * hd)
        xc = cb_ref[:, cols] + cw_ref[conv_width - 1:conv_width, cols] * u_ref[:, cols]
        for k in range(conv_width - 1):
            back = conv_width - 1 - k
            xc = xc + cw_ref[k:k + 1, cols] * ext_ref[pl.ds(8 - back, tc), cols]
        pre = jnp.dot(xc.astype(BF16), wax_ref[hh], preferred_element_type=F32) + bax_ref[hh]
        r = _sigmoid(pre[:, :hd])
        i = _sigmoid(pre[:, hd:])
        a = jnp.exp((-RG_C) * r * sp_ref[:, cols])
        b = jnp.sqrt(1.0 - a * a) * i * xc
        a3, b3 = _group_scan_real(a.reshape(ng, 8, hd), b.reshape(ng, 8, hd))
        a_ref[:, cols] = a3.reshape(tc, hd)
        b_ref[:, cols] = b3.reshape(tc, hd)
    ext_ref[0:8, :] = ext_ref[tc:tc + 8, :]
    y_ref[...] = _gelu_tanh(gate_ref[...])

    def body(g, hc):
        r0 = pl.multiple_of(g * 8, 8)
        h = a_ref[pl.ds(r0, 8), :] * hc + b_ref[pl.ds(r0, 8), :]
        y_ref[pl.ds(r0, 8), :] = h * y_ref[pl.ds(r0, 8), :]
        return jnp.broadcast_to(h[7:8, :], (8, width))

    hc = lax.fori_loop(0, ng, body, hc_ref[...], unroll=2)
    hc_ref[...] = hc

    @pl.when(t == pl.num_programs(2) - 1)
    def _():
        ht_ref[...] = hc
        tailt_ref[...] = ext_ref[0:8, :]


def _rg_lru(proj, conv_w, conv_b, wax, bax, softplus_neg_lam, h0, tail0, *, d_rg):
    bsz, t_len, _ = proj.shape
    heads, hd, _ = wax.shape
    conv_width = conv_w.shape[0]
    tc = _divisor_tile(t_len, 512, V7X_SUBLANES)
    hpc = _divisor_tile(heads, max(1, 1024 // hd), 1)
    wd = hpc * hd
    ncell = heads // hpc
    vmem = 2 * 3 * tc * wd * 4 + (tc + 8) * wd * 4 + 2 * tc * wd * 4 + 2 * hpc * hd * 2 * hd * 2 + 8 * tc * hd * 4
    body = functools.partial(_rg_body, conv_width=conv_width)
    hsel = lambda b, h, t: (0, 0, h)
    return pl.pallas_call(
        body,
        grid=(bsz, ncell, t_len // tc),
        in_specs=[
            pl.BlockSpec((None, tc, wd), lambda b, h, t: (b, t, h)),
            pl.BlockSpec((None, tc, wd), lambda b, h, t: (b, t, ncell + h)),
            pl.BlockSpec((conv_width, wd), lambda b, h, t: (0, h)),
            pl.BlockSpec((1, wd), lambda b, h, t: (0, h)),
            pl.BlockSpec((hpc, hd, 2 * hd), lambda b, h, t: (h, 0, 0)),
            pl.BlockSpec((hpc, 1, 2 * hd), lambda b, h, t: (h, 0, 0)),
            pl.BlockSpec((1, wd), lambda b, h, t: (0, h)),
            pl.BlockSpec((None, 8, wd), hsel),
            pl.BlockSpec((None, 8, wd), hsel),
        ],
        out_specs=[
            pl.BlockSpec((None, tc, wd), lambda b, h, t: (b, t, h)),
            pl.BlockSpec((None, 8, wd), lambda b, h, t: (b, 0, h)),
            pl.BlockSpec((None, 8, wd), lambda b, h, t: (b, 0, h)),
        ],
        out_shape=[
            jax.ShapeDtypeStruct((bsz, t_len, d_rg), F32),
            jax.ShapeDtypeStruct((bsz, 8, d_rg), F32),
            jax.ShapeDtypeStruct((bsz, 8, d_rg), F32),
        ],
        scratch_shapes=[
            pltpu.VMEM((tc + 8, wd), F32),
            pltpu.VMEM((tc, wd), F32),
            pltpu.VMEM((tc, wd), F32),
            pltpu.VMEM((8, wd), F32),
        ],
        compiler_params=_params(("parallel", "parallel", "arbitrary"), vmem),
        name="rg_lru",
    )(proj, proj, conv_w, conv_b.reshape(1, d_rg), wax, bax, softplus_neg_lam.reshape(1, d_rg),
      h0, tail0)


def _s5_body(u_ref, km_ref, wsm_ref, vm_ref, e1_ref, e2_ref, mm_ref, mw_ref, mv_ref, lam_ref, p8_ref, d_ref,
             x0_ref, z_ref, xt_ref, mi_ref, ws_ref, v_ref, s_ref, xc_ref):
    t = pl.program_id(2)
    tc, slab = u_ref.shape
    nc = tc // S5_CHUNK
    ns = lam_ref.shape[1] // 2
    ng = nc // V7X_SUBLANES

    @pl.when(jnp.logical_and(pl.program_id(1) == 0, t == 0))
    def _():
        def expand(small_ref, e_ref, mask_ref, dst_ref):
            width = dst_ref.shape[1]
            piece = min(width, 2 * V7X_LANES)
            small = small_ref[...].astype(BF16)
            for q0 in range(0, width, piece):
                full = jnp.dot(small, e_ref[:, q0:q0 + piece], preferred_element_type=F32)
                dst_ref[:, q0:q0 + piece] = (full * mask_ref[:, q0:q0 + piece].astype(F32)).astype(BF16)

        expand(km_ref, e1_ref, mm_ref, mi_ref)
        expand(wsm_ref, e2_ref, mw_ref, ws_ref)
        expand(vm_ref, e1_ref, mv_ref, v_ref)

    @pl.when(t == 0)
    def _():
        xc_ref[...] = x0_ref[...]

    us = [u_ref[pl.ds(i, nc, stride=S5_CHUNK), :] for i in range(S5_CHUNK)]
    ucat = jnp.concatenate([ui.astype(BF16) for ui in us], axis=-1)
    send = jnp.dot(ucat, ws_ref[...], preferred_element_type=F32)
    ycat = jnp.dot(ucat, mi_ref[...], preferred_element_type=F32)

    xr, xi = _group_scan_const_complex(send[:, :ns].reshape(ng, 8, ns), send[:, ns:].reshape(ng, 8, ns),
                                       lam_ref, ns)
    s_ref[7:8, :] = xc_ref[0:1, :]
    s_ref[8:8 + nc, :ns] = xr.reshape(nc, ns)
    s_ref[8:8 + nc, ns:] = xi.reshape(nc, ns)

    p8r = p8_ref[:, :ns]
    p8i = p8_ref[:, ns:]

    def body(g, carry):
        cr, ci = carry
        r0 = pl.multiple_of(8 + g * 8, 8)
        nr = s_ref[pl.ds(r0, 8), :ns] + (p8r * cr - p8i * ci)
        ni = s_ref[pl.ds(r0, 8), ns:] + (p8r * ci + p8i * cr)
        s_ref[pl.ds(r0, 8), :ns] = nr
        s_ref[pl.ds(r0, 8), ns:] = ni
        return (jnp.broadcast_to(nr[7:8, :], (8, ns)), jnp.broadcast_to(ni[7:8, :], (8, ns)))

    cr, ci = lax.fori_loop(0, ng, body, (xc_ref[:, :ns], xc_ref[:, ns:]), unroll=True)
    xc_ref[:, :ns] = cr
    xc_ref[:, ns:] = ci

    x_in = s_ref[pl.ds(7, nc), :]
    ycat = ycat + jnp.dot(x_in.astype(BF16), v_ref[...], preferred_element_type=F32)
    for i in range(S5_CHUNK):
        y = ycat[:, i * slab:(i + 1) * slab] + d_ref[...] * us[i]
        z_ref[pl.ds(i, nc, stride=S5_CHUNK), :] = _gelu_tanh(y)

    @pl.when(t == pl.num_programs(2) - 1)
    def _():
        xt_ref[...] = xc_ref[...]


def _s5_scan(proj, tables, consts, d_skip, x0, *, col0, d_s5):
    bsz, t_len, _ = proj.shape
    k_m, w_m, v_m, lam_pow, p8 = tables
    e1, e2, mask_m, mask_w, mask_v = consts
    nslab, kdim, _ = k_m.shape
    slab = kdim // S5_CHUNK
    ns2 = lam_pow.shape[-1]
    tc = _divisor_tile(t_len, 4096, S5_CHUNK * V7X_SUBLANES)
    assert tc % (S5_CHUNK * V7X_SUBLANES) == 0, (t_len, tc)
    nc = tc // S5_CHUNK
    cb0 = col0 // slab
    big = 2 * kdim * kdim + kdim * ns2
    vmem = (4 * tc * slab * 4 + 2 * big * 2 + big * 2 + 2 * (2 * kdim + ns2) * slab * 4 + (nc + 8) * ns2 * 4
            + nc * (2 * kdim * 4 + kdim * 2 + 3 * ns2 * 4))
    slab_blk = lambda a: pl.BlockSpec((None,) + a.shape[1:], lambda s, b, t: (s, 0, 0))
    const_blk = lambda a: pl.BlockSpec(a.shape, lambda s, b, t: (0, 0))
    return pl.pallas_call(
        _s5_body,
        grid=(nslab, bsz, t_len // tc),
        in_specs=[
            pl.BlockSpec((None, tc, slab), lambda s, b, t: (b, t, cb0 + s)),
            slab_blk(k_m), slab_blk(w_m), slab_blk(v_m),
            const_blk(e1), const_blk(e2), const_blk(mask_m), const_blk(mask_w), const_blk(mask_v),
            pl.BlockSpec((None, 8, ns2), lambda s, b, t: (s, 0, 0)),
            pl.BlockSpec((None, 8, ns2), lambda s, b, t: (s, 0, 0)),
            pl.BlockSpec((1, slab), lambda s, b, t: (0, s)),
            pl.BlockSpec((None, None, 8, ns2), lambda s, b, t: (0, s, 0, 0)),
        ],
        out_specs=[
            pl.BlockSpec((None, tc, slab), lambda s, b, t: (b, t, s)),
            pl.BlockSpec((None, None, 8, ns2), lambda s, b, t: (b, s, 0, 0)),
        ],
        out_shape=[
            jax.ShapeDtypeStruct((bsz, t_len, d_s5), F32),
            jax.ShapeDtypeStruct((bsz, nslab, 8, ns2), F32),
        ],
        scratch_shapes=[
            pltpu.VMEM((kdim, kdim), BF16),
            pltpu.VMEM((kdim, ns2), BF16),
            pltpu.VMEM((ns2, kdim), BF16),
            pltpu.VMEM((nc + 8, ns2), F32),
            pltpu.VMEM((8, ns2), F32),
        ],
        compiler_params=_params(("parallel", "arbitrary", "arbitrary"), vmem),
        name="s5_scan",
    )(proj, k_m, w_m, v_m, e1, e2, mask_m, mask_w, mask_v, lam_pow, p8, d_skip.reshape(1, d_s5), x0)


def _glu_body(z_ref, w_ref, b_ref, o_ref, zb_ref):
    tm = z_ref.shape[0]
    tn = o_ref.shape[1]
    rc = _row_chunk(tm)

    @pl.when(pl.program_id(1) == 0)
    def _():
        def body(c, _):
            r0 = pl.multiple_of(c * rc, rc)
            zb_ref[pl.ds(r0, rc), :] = z_ref[pl.ds(r0, rc), :].astype(BF16)
            return 0

        lax.fori_loop(0, tm // rc, body, 0, unroll=2)

    pre = jnp.dot(zb_ref[...], w_ref[...], preferred_element_type=F32) + b_ref[...]
    cols = pl.ds(pl.multiple_of(pl.program_id(1) * tn, tn), tn)
    o_ref[...] = z_ref[:, cols] * _sigmoid(pre)


def _glu(z, w, b):
    m, d = z.shape
    tm = _divisor_tile(m, 1024, 16)
    tn = _divisor_tile(d, 1024, V7X_LANES)
    vmem = 2 * tm * d * 4 + tm * d * 2 + 2 * d * tn * 2 + 4 * tm * tn * 4
    return pl.pallas_call(
        _glu_body,
        grid=(m // tm, d // tn),
        in_specs=[
            pl.BlockSpec((tm, d), lambda i, j: (i, 0)),
            pl.BlockSpec((d, tn), lambda i, j: (0, j)),
            pl.BlockSpec((1, tn), lambda i, j: (0, j)),
        ],
        out_specs=pl.BlockSpec((tm, tn), lambda i, j: (i, j)),
        out_shape=jax.ShapeDtypeStruct((m, d), F32),
        scratch_shapes=[pltpu.VMEM((tm, d), BF16)],
        compiler_params=_params(("parallel", "arbitrary"), vmem),
        name="s5_glu",
    )(z, w, b.reshape(1, d))


def _outproj_body(yr_hbm, ys_hbm, gr_ref, gs_ref, w_ref, h_ref, o_ref, yn_ref, rbuf, sbuf, rsem, ssem,
                  *, n_chunks, ck):
    tm = yn_ref.shape[0]
    d_rg = rbuf.shape[-1]
    rc = _row_chunk(ck)

    @pl.when(pl.program_id(1) == 0)
    def _():
        def consume(c, views):
            def body(r, _):
                r0 = pl.multiple_of(r * rc, rc)
                dst = pl.ds(c * ck + r0, rc)
                yn_ref[dst, :d_rg] = _rms_rows(views[0][pl.ds(r0, rc), :], gr_ref[...]).astype(BF16)
                yn_ref[dst, d_rg:] = _rms_rows(views[1][pl.ds(r0, rc), :], gs_ref[...]).astype(BF16)
                return 0

            lax.fori_loop(0, ck // rc, body, 0, unroll=2)

        _stream_rows([yr_hbm, ys_hbm], [rbuf, sbuf], [rsem, ssem], pl.program_id(0) * tm, ck, n_chunks, consume)

    o_ref[...] = h_ref[...] + jnp.dot(yn_ref[...], w_ref[...], preferred_element_type=F32)


def _outproj(y_rg, y_s5, g_rg, g_s5, w, h):
    m, d_rg = y_rg.shape
    d_s5 = y_s5.shape[1]
    d_mix, d = w.shape
    tm = _divisor_tile(m, 1024, 16)
    tn = _divisor_tile(d, 1024, V7X_LANES)
    n_chunks, ck = _chunks(tm)
    vmem = 2 * ck * d_mix * 4 + tm * d_mix * 2 + 2 * d_mix * tn * 2 + 5 * tm * tn * 4
    return pl.pallas_call(
        functools.partial(_outproj_body, n_chunks=n_chunks, ck=ck),
        grid=(m // tm, d // tn),
        in_specs=[
            pl.BlockSpec(memory_space=pl.ANY),
            pl.BlockSpec(memory_space=pl.ANY),
            pl.BlockSpec((1, d_rg), lambda i, j: (0, 0)),
            pl.BlockSpec((1, d_s5), lambda i, j: (0, 0)),
            pl.BlockSpec((d_mix, tn), lambda i, j: (0, j)),
            pl.BlockSpec((tm, tn), lambda i, j: (i, j)),
        ],
        out_specs=pl.BlockSpec((tm, tn), lambda i, j: (i, j)),
        out_shape=jax.ShapeDtypeStruct((m, d), F32),
        scratch_shapes=[
            pltpu.VMEM((tm, d_mix), BF16),
            pltpu.VMEM((2, ck, d_rg), F32),
            pltpu.VMEM((2, ck, d_s5), F32),
            pltpu.SemaphoreType.DMA((2,)),
            pltpu.SemaphoreType.DMA((2,)),
        ],
        compiler_params=_params(("parallel", "arbitrary"), vmem),
        name="out_proj",
    )(y_rg, y_s5, g_rg.reshape(1, d_rg), g_s5.reshape(1, d_s5), w, h)


def _s5_tables(lam_re, lam_im, log_dt, b_re, b_im, c_re, c_im, slab):
    g, n = lam_re.shape
    c = b_re.shape[-1]
    gs = slab // c
    nslab = g // gs
    ell = S5_CHUNK
    dt = jnp.exp(log_dt.astype(F32))[:, None]
    lam = lax.complex(lam_re.astype(F32), lam_im.astype(F32))
    lam_dt = lam * dt
    lam_bar = jnp.exp(lam_dt)
    b_bar = ((lam_bar - 1.0) / lam)[..., None] * lax.complex(b_re.astype(F32), b_im.astype(F32))
    cc = lax.complex(c_re.astype(F32), c_im.astype(F32))

    def lam_pow(ks):
        return jnp.exp(lam_dt[None] * jnp.asarray(ks, F32)[:, None, None])

    def state_lanes(p):
        p = p.reshape(p.shape[0], nslab, gs * n)
        return jnp.concatenate([jnp.real(p), jnp.imag(p)], axis=-1).transpose(1, 0, 2)

    lam_tab = state_lanes(lam_pow([ell, 2 * ell, 4 * ell] + [ell] * 5))
    p8 = state_lanes(lam_pow([ell * (r + 1) for r in range(8)]))

    pw = lam_pow(list(range(ell + 1)))
    lag = np.arange(ell)[None, :] - np.arange(ell)[:, None]
    causal = jnp.asarray(lag >= 0, F32)[:, :, None, None]
    pw_lag = pw[np.clip(lag, 0, ell - 1)] * causal
    pw_lag = pw_lag.reshape(ell, ell, nslab, gs, n)
    cc_s = cc.reshape(nslab, gs, c, n)
    bb_s = b_bar.reshape(nslab, gs, n, c)
    k_m = jnp.real(jnp.einsum('sgdn,ijsgn,sgnc->sigcjd', cc_s, pw_lag, bb_s)).reshape(nslab, ell * gs * c, ell * c)
    wst = pw[ell - 1 - jnp.arange(ell)][..., None] * b_bar[None]
    wst = wst.reshape(ell, nslab, gs, n, c).transpose(1, 0, 2, 4, 3).reshape(nslab, ell * gs * c, n)
    w_m = jnp.concatenate([jnp.real(wst), jnp.imag(wst)], axis=-1)
    gg = cc[None] * pw[1:ell + 1][:, :, None, :]
    gg = gg.reshape(ell, nslab, gs, c, n).transpose(1, 2, 4, 0, 3).reshape(nslab, gs * n, ell * c)
    v_m = jnp.concatenate([jnp.real(gg), -jnp.imag(gg)], axis=1)
    return k_m, w_m, v_m, lam_tab, p8


def _s5_spread_consts(gs, c, n):
    ell = S5_CHUNK
    a1, q1 = np.arange(ell * c)[:, None], np.arange(ell * gs * c)[None, :]
    e1 = (a1 // c == q1 // (gs * c)) & (a1 % c == q1 % c)
    a2, q2 = np.arange(2 * n)[:, None], np.arange(2 * gs * n)[None, :]
    e2 = (a2 // n == q2 // (gs * n)) & (a2 % n == q2 % n)
    g_in = (np.arange(ell * gs * c) // c) % gs
    g_st = (np.arange(2 * gs * n) % (gs * n)) // n
    mask_m = g_in[:, None] == g_in[None, :]
    mask_w = g_in[:, None] == g_st[None, :]
    mask_v = g_st[:, None] == g_in[None, :]
    return tuple(jnp.asarray(m, dtype=BF16) for m in (e1, e2, mask_m, mask_w, mask_v))


def kernel(x, meta_tokens, ffn1_norm, ffn1_w_gate, ffn1_w_up, ffn1_w_down, mix_norm, w_in,
           rg_conv_w, rg_conv_b, rg_w_a, rg_b_a, rg_w_x, rg_b_x, rg_lambda,
           s5_lambda_re, s5_lambda_im, s5_log_dt, s5_b_re, s5_b_im, s5_c_re, s5_c_im, s5_d,
           s5_glu_w, s5_glu_b, rg_out_norm, s5_out_norm, w_out,
           ffn2_norm, ffn2_w_gate, ffn2_w_up, ffn2_w_down, final_norm):
    bsz, t_len, d = x.shape
    depth = ffn1_norm.shape[0]
    d_rg = rg_lambda.shape[-1]
    d_s5 = s5_d.shape[-1]
    heads, hd = rg_w_a.shape[1], rg_w_a.shape[2]
    slab = V7X_LANES
    s5_align = S5_CHUNK * V7X_SUBLANES

    h = x.reshape(bsz * t_len, d)
    hm = meta_tokens.astype(x.dtype)
    n_meta = hm.shape[0]
    for l in range(depth):
        last = l == depth - 1
        ffn1_w = (ffn1_w_gate[l].astype(BF16), ffn1_w_up[l].astype(BF16), ffn1_w_down[l].astype(BF16))
        ffn2_w = (ffn2_w_gate[l].astype(BF16), ffn2_w_up[l].astype(BF16), ffn2_w_down[l].astype(BF16))
        w_in_b = w_in[l].astype(BF16)
        w_out_b = w_out[l].astype(BF16)
        glu_w_b = s5_glu_w[l].astype(BF16)
        wax = jnp.concatenate([rg_w_a[l], rg_w_x[l]], axis=-1).astype(BF16)
        bax = jnp.concatenate([rg_b_a[l], rg_b_x[l]], axis=-1).reshape(heads, 1, 2 * hd)
        sp = jax.nn.softplus(-rg_lambda[l].astype(F32))
        s5_mats = _s5_tables(s5_lambda_re[l], s5_lambda_im[l], s5_log_dt[l], s5_b_re[l], s5_b_im[l],
                             s5_c_re[l], s5_c_im[l], slab)
        nslab, _, ns2 = s5_mats[3].shape
        s5_consts = _s5_spread_consts(slab // s5_b_re.shape[-1], s5_b_re.shape[-1], s5_lambda_re.shape[-1])
        fnorm = final_norm if last else ffn2_norm[l]

        def mixers(hrows, nb, h0, tail0, x0, front_pad):
            h1 = _ffn(hrows, ffn1_norm[l], *ffn1_w, ffn1_norm[l], final_norm=False)
            proj = _inproj(h1, mix_norm[l], w_in_b).reshape(nb, hrows.shape[0] // nb, -1)
            y_rg, h_t, tail_t = _rg_lru(proj, rg_conv_w[l], rg_conv_b[l], wax, bax, sp, h0, tail0, d_rg=d_rg)
            proj_s5 = jnp.pad(proj, ((0, 0), (front_pad, 0), (0, 0))) if front_pad else proj
            z, x_t = _s5_scan(proj_s5, s5_mats, s5_consts, s5_d[l], x0, col0=2 * d_rg, d_s5=d_s5)
            return h1, y_rg, z[:, front_pad:], h_t, tail_t, x_t

        def tail_layers(h1, y_rg, z):
            m = h1.shape[0]
            y_s5 = _glu(z.reshape(m, d_s5), glu_w_b, s5_glu_b[l])
            h2 = _outproj(y_rg.reshape(m, d_rg), y_s5, rg_out_norm[l], s5_out_norm[l], w_out_b, h1)
            return _ffn(h2, ffn2_norm[l], *ffn2_w, fnorm, final_norm=last)

        assert t_len % s5_align == 0, t_len
        zeros_rg = jnp.zeros((1, 8, d_rg), F32)
        zeros_s5 = jnp.zeros((1, nslab, 8, ns2), F32)
        meta_pad = (-n_meta) % s5_align
        h1m, y_rgm, zm, h_t, tail_t, x_t = mixers(hm, 1, zeros_rg, zeros_rg, zeros_s5, meta_pad)
        h1, y_rg, z, _, _, _ = mixers(h, bsz, h_t, tail_t, x_t, 0)
        h = tail_layers(h1, y_rg, z)
        if not last:
            hm = tail_layers(h1m, y_rgm, zm)
    return h.reshape(bsz, t_len, d)
```

```python
import functools
import math

import jax
import jax.numpy as jnp
import numpy as np
from jax import lax
from jax.experimental import pallas as pl
from jax.experimental.pallas import tpu as pltpu

EPS = 1e-6
RG_C = 8.0
F32 = jnp.float32
BF16 = jnp.bfloat16

V7X_LANES = 128
V7X_SUBLANES = 8
V7X_VMEM_BYTES = 64 * 1024 * 1024

S5_CHUNK = 8


def _divisor_tile(n, pref, align):
    if n <= pref:
        return n
    t = (pref // align) * align
    while t >= align:
        if n % t == 0:
            return t
        t -= align
    return n


def _params(sem, vmem_bytes):
    limit = int(min(V7X_VMEM_BYTES - (2 << 20), max(vmem_bytes + (6 << 20), 32 << 20)))
    return pltpu.CompilerParams(dimension_semantics=sem, vmem_limit_bytes=limit)


def _rms_rows(x, g):
    ms = jnp.mean(x * x, axis=-1, keepdims=True)
    return x * lax.rsqrt(ms + EPS) * g


def _sigmoid(x):
    return 1.0 / (1.0 + jnp.exp(-x))


def _gelu_tanh(x):
    c = math.sqrt(2.0 / math.pi)
    return 0.5 * x * (1.0 + jnp.tanh(c * (x + 0.044715 * (x * x * x))))


def _row_chunk(tm):
    return _divisor_tile(tm, 32, V7X_SUBLANES)


def _ffn_body(x_hbm, g_ref, wg_ref, wu_ref, wd_ref, g2_ref, o_ref, hn_ref, sem, *, n_split, final_norm, n_chunks):
    i = pl.program_id(0)
    j = pl.program_id(1)
    tm, d = o_ref.shape
    ck = tm // n_chunks
    rc = _row_chunk(ck)

    def x_copy(c):
        rows = pl.ds(pl.multiple_of(i * tm + c * ck, ck), ck)
        return pltpu.make_async_copy(x_hbm.at[rows, :], o_ref.at[pl.ds(c * ck, ck), :], sem.at[c])

    @pl.when(j == 0)
    def _():
        for c in range(n_chunks):
            x_copy(c).start()
        for c in range(n_chunks):
            x_copy(c).wait()

            def body(r, _):
                r0 = pl.multiple_of(c * ck + r * rc, rc)
                x = o_ref[pl.ds(r0, rc), :]
                hn_ref[pl.ds(r0, rc), :] = _rms_rows(x, g_ref[...]).astype(BF16)
                o_ref[pl.ds(r0, rc), :] = x + x
                return 0

            lax.fori_loop(0, ck // rc, body, 0, unroll=2)

    g = jnp.dot(hn_ref[...], wg_ref[...], preferred_element_type=F32)
    u = jnp.dot(hn_ref[...], wu_ref[...], preferred_element_type=F32)
    a = (g * _sigmoid(g) * u).astype(BF16)
    dn = d // n_split
    for s in range(n_split):
        o_ref[:, s * dn:(s + 1) * dn] += jnp.dot(a, wd_ref[:, s * dn:(s + 1) * dn].astype(BF16),
                                                 preferred_element_type=F32)

    @pl.when(j == pl.num_programs(1) - 1)
    def _():
        def body(c, _):
            r0 = pl.multiple_of(c * rc, rc)
            h = 0.5 * o_ref[pl.ds(r0, rc), :]
            if final_norm:
                h = _rms_rows(h, g2_ref[...])
            o_ref[pl.ds(r0, rc), :] = h
            return 0

        lax.fori_loop(0, tm // rc, body, 0, unroll=2)


def _ffn(x, norm_w, wg, wu, wd, norm2_w, *, final_norm):
    m, d = x.shape
    f = wg.shape[1]
    tf = _divisor_tile(f, 256, V7X_LANES)
    tm = _divisor_tile(m, 1024, 16)
    n_chunks = max(1, tm // 128)
    n_split = max(1, d // 1024)
    vmem = 2 * tm * d * 4 + tm * d * 2 + 2 * d * tf * (2 * 2 + wd.dtype.itemsize) + tm * tf * 4 * 4
    body = functools.partial(_ffn_body, n_split=n_split, final_norm=final_norm, n_chunks=n_chunks)
    return pl.pallas_call(
        body,
        grid=(m // tm, f // tf),
        in_specs=[
            pl.BlockSpec(memory_space=pl.ANY),
            pl.BlockSpec((1, d), lambda i, j: (0, 0)),
            pl.BlockSpec((d, tf), lambda i, j: (0, j)),
            pl.BlockSpec((d, tf), lambda i, j: (0, j)),
            pl.BlockSpec((tf, d), lambda i, j: (j, 0)),
            pl.BlockSpec((1, d), lambda i, j: (0, 0)),
        ],
        out_specs=pl.BlockSpec((tm, d), lambda i, j: (i, 0)),
        out_shape=jax.ShapeDtypeStruct((m, d), F32),
        scratch_shapes=[pltpu.VMEM((tm, d), BF16), pltpu.SemaphoreType.DMA((n_chunks,))],
        compiler_params=_params(("parallel", "arbitrary"), vmem),
        name="ffn_swiglu",
    )(x, norm_w.reshape(1, d), wg, wu, wd, norm2_w.reshape(1, d))


def _staged_rows(srcs, bufs, sems, consume):
    i = pl.program_id(0)
    j = pl.program_id(1)
    tm = bufs[0].shape[0]

    def copies(tile):
        rows = pl.ds(pl.multiple_of(tile * tm, tm), tm)
        return [pltpu.make_async_copy(s.at[rows, :], b, m.at[0]) for s, b, m in zip(srcs, bufs, sems)]

    @pl.when(j == 0)
    def _():
        @pl.when(i == 0)
        def _():
            for cp in copies(0):
                cp.start()

        for cp in copies(i):
            cp.wait()
        consume()

    @pl.when(jnp.logical_and(j == 1, i + 1 < pl.num_programs(0)))
    def _():
        for cp in copies(i + 1):
            cp.start()


def _inproj_body(x_hbm, g_ref, w_ref, o_ref, hn_ref, xbuf, sem):
    tm = hn_ref.shape[0]
    rc = _row_chunk(tm)

    def consume():
        def body(c, _):
            r0 = pl.multiple_of(c * rc, rc)
            hn_ref[pl.ds(r0, rc), :] = _rms_rows(xbuf[pl.ds(r0, rc), :], g_ref[...]).astype(BF16)
            return 0

        lax.fori_loop(0, tm // rc, body, 0, unroll=2)

    _staged_rows([x_hbm], [xbuf], [sem], consume)
    o_ref[...] = jnp.dot(hn_ref[...], w_ref[...], preferred_element_type=F32)


def _inproj(x, norm_w, w):
    m, d = x.shape
    n = w.shape[1]
    tm = _divisor_tile(m, 1024, 16)
    tn = _divisor_tile(n, min(1024, n // 2), V7X_LANES)
    assert n // tn >= 2, (n, tn)
    vmem = tm * d * 4 + tm * d * 2 + 2 * d * tn * 2 + 3 * tm * tn * 4
    return pl.pallas_call(
        _inproj_body,
        grid=(m // tm, n // tn),
        in_specs=[
            pl.BlockSpec(memory_space=pl.ANY),
            pl.BlockSpec((1, d), lambda i, j: (0, 0)),
            pl.BlockSpec((d, tn), lambda i, j: (0, j)),
        ],
        out_specs=pl.BlockSpec((tm, tn), lambda i, j: (i, j)),
        out_shape=jax.ShapeDtypeStruct((m, n), F32),
        scratch_shapes=[pltpu.VMEM((tm, d), BF16), pltpu.VMEM((tm, d), F32), pltpu.SemaphoreType.DMA((1,))],
        compiler_params=_params(("arbitrary", "arbitrary"), vmem),
        name="in_proj",
    )(x, norm_w.reshape(1, d), w)


def _group_scan_real(a, b):
    row = lax.broadcasted_iota(jnp.int32, a.shape, 1)
    for dist in (1, 2, 4):
        keep = row >= dist
        a_prev = jnp.where(keep, pltpu.roll(a, dist, axis=1), 1.0)
        b_prev = jnp.where(keep, pltpu.roll(b, dist, axis=1), 0.0)
        b = a * b_prev + b
        a = a * a_prev
    return a, b


def _group_scan_const_complex(xr, xi, lam_ref, ns):
    row = lax.broadcasted_iota(jnp.int32, xr.shape, 1)
    for k, dist in enumerate((1, 2, 4)):
        lr = lam_ref[k:k + 1, :ns]
        li = lam_ref[k:k + 1, ns:]
        keep = row >= dist
        pr = jnp.where(keep, pltpu.roll(xr, dist, axis=1), 0.0)
        pi = jnp.where(keep, pltpu.roll(xi, dist, axis=1), 0.0)
        xr, xi = xr + (lr * pr - li * pi), xi + (lr * pi + li * pr)
    return xr, xi


def _rg_body(u_ref, gate_ref, cw_ref, cb_ref, wax_ref, bax_ref, sp_ref, h0_ref, tail0_ref,
             y_ref, ht_ref, tailt_ref, ext_ref, a_ref, b_ref, hc_ref, *, conv_width):
    t = pl.program_id(2)
    tc, width = u_ref.shape
    hpc, hd, _ = wax_ref.shape
    ng = tc // V7X_SUBLANES

    @pl.when(t == 0)
    def _():
        ext_ref[0:8, :] = tail0_ref[...]
        hc_ref[...] = h0_ref[...]

    ext_ref[8:8 + tc, :] = u_ref[...]
    for hh in range(hpc):
        cols = slice(hh * hd, (hh + 1) * hd)
        xc = cb_ref[:, cols] + cw_ref[conv_width - 1:conv_width, cols] * u_ref[:, cols]
        for k in range(conv_width - 1):
            back = conv_width - 1 - k
            xc = xc + cw_ref[k:k + 1, cols] * ext_ref[pl.ds(8 - back, tc), cols]
        pre = jnp.dot(xc.astype(BF16), wax_ref[hh], preferred_element_type=F32) + bax_ref[hh]
        r = _sigmoid(pre[:, :hd])
        i = _sigmoid(pre[:, hd:])
        a = jnp.exp((-RG_C) * r * sp_ref[:, cols])
        b = jnp.sqrt(1.0 - a * a) * i * xc
        a3, b3 = _group_scan_real(a.reshape(ng, 8, hd), b.reshape(ng, 8, hd))
        a_ref[:, cols] = a3.reshape(tc, hd)
        b_ref[:, cols] = b3.reshape(tc, hd)
    ext_ref[0:8, :] = ext_ref[tc:tc + 8, :]
    y_ref[...] = _gelu_tanh(gate_ref[...])

    def body(g, hc):
        r0 = pl.multiple_of(g * 8, 8)
        h = a_ref[pl.ds(r0, 8), :] * hc + b_ref[pl.ds(r0, 8), :]
        y_ref[pl.ds(r0, 8), :] = h * y_ref[pl.ds(r0, 8), :]
        return jnp.broadcast_to(h[7:8, :], (8, width))

    hc = lax.fori_loop(0, ng, body, hc_ref[...], unroll=2)
    hc_ref[...] = hc

    @pl.when(t == pl.num_programs(2) - 1)
    def _():
        ht_ref[...] = hc
        tailt_ref[...] = ext_ref[0:8, :]


def _rg_lru(proj, conv_w, conv_b, wax, bax, softplus_neg_lam, h0, tail0, *, d_rg):
    bsz, t_len, _ = proj.shape
    heads, hd, _ = wax.shape
    conv_width = conv_w.shape[0]
    tc = _divisor_tile(t_len, 512, V7X_SUBLANES)
    hpc = _divisor_tile(heads, max(1, 1024 // hd), 1)
    wd = hpc * hd
    ncell = heads // hpc
    vmem = 2 * 3 * tc * wd * 4 + (tc + 8) * wd * 4 + 2 * tc * wd * 4 + 2 * hpc * hd * 2 * hd * 2 + 8 * tc * hd * 4
    body = functools.partial(_rg_body, conv_width=conv_width)
    hsel = lambda b, h, t: (0, 0, h)
    return pl.pallas_call(
        body,
        grid=(bsz, ncell, t_len // tc),
        in_specs=[
            pl.BlockSpec((None, tc, wd), lambda b, h, t: (b, t, h)),
            pl.BlockSpec((None, tc, wd), lambda b, h, t: (b, t, ncell + h)),
            pl.BlockSpec((conv_width, wd), lambda b, h, t: (0, h)),
            pl.BlockSpec((1, wd), lambda b, h, t: (0, h)),
            pl.BlockSpec((hpc, hd, 2 * hd), lambda b, h, t: (h, 0, 0)),
            pl.BlockSpec((hpc, 1, 2 * hd), lambda b, h, t: (h, 0, 0)),
            pl.BlockSpec((1, wd), lambda b, h, t: (0, h)),
            pl.BlockSpec((None, 8, wd), hsel),
            pl.BlockSpec((None, 8, wd), hsel),
        ],
        out_specs=[
            pl.BlockSpec((None, tc, wd), lambda b, h, t: (b, t, h)),
            pl.BlockSpec((None, 8, wd), lambda b, h, t: (b, 0, h)),
            pl.BlockSpec((None, 8, wd), lambda b, h, t: (b, 0, h)),
        ],
        out_shape=[
            jax.ShapeDtypeStruct((bsz, t_len, d_rg), F32),
            jax.ShapeDtypeStruct((bsz, 8, d_rg), F32),
            jax.ShapeDtypeStruct((bsz, 8, d_rg), F32),
        ],
        scratch_shapes=[
            pltpu.VMEM((tc + 8, wd), F32),
            pltpu.VMEM((tc, wd), F32),
            pltpu.VMEM((tc, wd), F32),
            pltpu.VMEM((8, wd), F32),
        ],
        compiler_params=_params(("parallel", "parallel", "arbitrary"), vmem),
        name="rg_lru",
    )(proj, proj, conv_w, conv_b.reshape(1, d_rg), wax, bax, softplus_neg_lam.reshape(1, d_rg),
      h0, tail0)


def _s5_body(u_ref, km_ref, wsm_ref, vm_ref, e1_ref, e2_ref, mm_ref, mw_ref, mv_ref, lam_ref, p8_ref, d_ref,
             x0_ref, z_ref, xt_ref, mi_ref, ws_ref, v_ref, s_ref, xc_ref):
    t = pl.program_id(2)
    tc, slab = u_ref.shape
    nc = tc // S5_CHUNK
    ns = lam_ref.shape[1] // 2
    ng = nc // V7X_SUBLANES

    @pl.when(jnp.logical_and(pl.program_id(1) == 0, t == 0))
    def _():
        def expand(small_ref, e_ref, mask_ref, dst_ref):
            width = dst_ref.shape[1]
            piece = min(width, 2 * V7X_LANES)
            small = small_ref[...].astype(BF16)
            for q0 in range(0, width, piece):
                full = jnp.dot(small, e_ref[:, q0:q0 + piece], preferred_element_type=F32)
                dst_ref[:, q0:q0 + piece] = (full * mask_ref[:, q0:q0 + piece].astype(F32)).astype(BF16)

        expand(km_ref, e1_ref, mm_ref, mi_ref)
        expand(wsm_ref, e2_ref, mw_ref, ws_ref)
        expand(vm_ref, e1_ref, mv_ref, v_ref)

    @pl.when(t == 0)
    def _():
        xc_ref[...] = x0_ref[...]

    us = [u_ref[pl.ds(i, nc, stride=S5_CHUNK), :] for i in range(S5_CHUNK)]
    ucat = jnp.concatenate([ui.astype(BF16) for ui in us], axis=-1)
    send = jnp.dot(ucat, ws_ref[...], preferred_element_type=F32)
    ycat = jnp.dot(ucat, mi_ref[...], preferred_element_type=F32)

    xr, xi = _group_scan_const_complex(send[:, :ns].reshape(ng, 8, ns), send[:, ns:].reshape(ng, 8, ns),
                                       lam_ref, ns)
    s_ref[7:8, :] = xc_ref[0:1, :]
    s_ref[8:8 + nc, :ns] = xr.reshape(nc, ns)
    s_ref[8:8 + nc, ns:] = xi.reshape(nc, ns)

    p8r = p8_ref[:, :ns]
    p8i = p8_ref[:, ns:]

    def body(g, carry):
        cr, ci = carry
        r0 = pl.multiple_of(8 + g * 8, 8)
        nr = s_ref[pl.ds(r0, 8), :ns] + (p8r * cr - p8i * ci)
        ni = s_ref[pl.ds(r0, 8), ns:] + (p8r * ci + p8i * cr)
        s_ref[pl.ds(r0, 8), :ns] = nr
        s_ref[pl.ds(r0, 8), ns:] = ni
        return (jnp.broadcast_to(nr[7:8, :], (8, ns)), jnp.broadcast_to(ni[7:8, :], (8, ns)))

    cr, ci = lax.fori_loop(0, ng, body, (xc_ref[:, :ns], xc_ref[:, ns:]), unroll=True)
    xc_ref[:, :ns] = cr
    xc_ref[:, ns:] = ci

    x_in = s_ref[pl.ds(7, nc), :]
    ycat = ycat + jnp.dot(x_in.astype(BF16), v_ref[...], preferred_element_type=F32)
    for i in range(S5_CHUNK):
        y = ycat[:, i * slab:(i + 1) * slab] + d_ref[...] * us[i]
        z_ref[pl.ds(i, nc, stride=S5_CHUNK), :] = _gelu_tanh(y)

    @pl.when(t == pl.num_programs(2) - 1)
    def _():
        xt_ref[...] = xc_ref[...]


def _s5_scan(proj, tables, consts, d_skip, x0, *, col0, d_s5):
    bsz, t_len, _ = proj.shape
    k_m, w_m, v_m, lam_pow, p8 = tables
    e1, e2, mask_m, mask_w, mask_v = consts
    nslab, kdim, _ = k_m.shape
    slab = kdim // S5_CHUNK
    ns2 = lam_pow.shape[-1]
    tc = _divisor_tile(t_len, 4096, S5_CHUNK * V7X_SUBLANES)
    assert tc % (S5_CHUNK * V7X_SUBLANES) == 0, (t_len, tc)
    nc = tc // S5_CHUNK
    cb0 = col0 // slab
    big = 2 * kdim * kdim + kdim * ns2
    vmem = (4 * tc * slab * 4 + 2 * big * 2 + big * 2 + 2 * (2 * kdim + ns2) * slab * 4 + (nc + 8) * ns2 * 4
            + nc * (2 * kdim * 4 + kdim * 2 + 3 * ns2 * 4))
    slab_blk = lambda a: pl.BlockSpec((None,) + a.shape[1:], lambda s, b, t: (s, 0, 0))
    const_blk = lambda a: pl.BlockSpec(a.shape, lambda s, b, t: (0, 0))
    return pl.pallas_call(
        _s5_body,
        grid=(nslab, bsz, t_len // tc),
        in_specs=[
            pl.BlockSpec((None, tc, slab), lambda s, b, t: (b, t, cb0 + s)),
            slab_blk(k_m), slab_blk(w_m), slab_blk(v_m),
            const_blk(e1), const_blk(e2), const_blk(mask_m), const_blk(mask_w), const_blk(mask_v),
            pl.BlockSpec((None, 8, ns2), lambda s, b, t: (s, 0, 0)),
            pl.BlockSpec((None, 8, ns2), lambda s, b, t: (s, 0, 0)),
            pl.BlockSpec((1, slab), lambda s, b, t: (0, s)),
            pl.BlockSpec((None, None, 8, ns2), lambda s, b, t: (0, s, 0, 0)),
        ],
        out_specs=[
            pl.BlockSpec((None, tc, slab), lambda s, b, t: (b, t, s)),
            pl.BlockSpec((None, None, 8, ns2), lambda s, b, t: (b, s, 0, 0)),
        ],
        out_shape=[
            jax.ShapeDtypeStruct((bsz, t_len, d_s5), F32),
            jax.ShapeDtypeStruct((bsz, nslab, 8, ns2), F32),
        ],
        scratch_shapes=[
            pltpu.VMEM((kdim, kdim), BF16),
            pltpu.VMEM((kdim, ns2), BF16),
            pltpu.VMEM((ns2, kdim), BF16),
            pltpu.VMEM((nc + 8, ns2), F32),
            pltpu.VMEM((8, ns2), F32),
        ],
        compiler_params=_params(("parallel", "arbitrary", "arbitrary"), vmem),
        name="s5_scan",
    )(proj, k_m, w_m, v_m, e1, e2, mask_m, mask_w, mask_v, lam_pow, p8, d_skip.reshape(1, d_s5), x0)


def _glu_body(z_ref, w_ref, b_ref, o_ref, zb_ref):
    tm = z_ref.shape[0]
    tn = o_ref.shape[1]
    rc = _row_chunk(tm)

    @pl.when(pl.program_id(1) == 0)
    def _():
        def body(c, _):
            r0 = pl.multiple_of(c * rc, rc)
            zb_ref[pl.ds(r0, rc), :] = z_ref[pl.ds(r0, rc), :].astype(BF16)
            return 0

        lax.fori_loop(0, tm // rc, body, 0, unroll=2)

    pre = jnp.dot(zb_ref[...], w_ref[...], preferred_element_type=F32) + b_ref[...]
    cols = pl.ds(pl.multiple_of(pl.program_id(1) * tn, tn), tn)
    o_ref[...] = z_ref[:, cols] * _sigmoid(pre)


def _glu(z, w, b):
    m, d = z.shape
    tm = _divisor_tile(m, 1024, 16)
    tn = _divisor_tile(d, 1024, V7X_LANES)
    vmem = 2 * tm * d * 4 + tm * d * 2 + 2 * d * tn * 2 + 4 * tm * tn * 4
    return pl.pallas_call(
        _glu_body,
        grid=(m // tm, d // tn),
        in_specs=[
            pl.BlockSpec((tm, d), lambda i, j: (i, 0)),
            pl.BlockSpec((d, tn), lambda i, j: (0, j)),
            pl.BlockSpec((1, tn), lambda i, j: (0, j)),
        ],
        out_specs=pl.BlockSpec((tm, tn), lambda i, j: (i, j)),
        out_shape=jax.ShapeDtypeStruct((m, d), F32),
        scratch_shapes=[pltpu.VMEM((tm, d), BF16)],
        compiler_params=_params(("parallel", "arbitrary"), vmem),
        name="s5_glu",
    )(z, w, b.reshape(1, d))


def _outproj_body(yr_hbm, ys_hbm, gr_ref, gs_ref, w_ref, h_ref, o_ref, yn_ref, rbuf, sbuf, rsem, ssem):
    tm = yn_ref.shape[0]
    d_rg = rbuf.shape[-1]
    rc = _row_chunk(tm)

    def consume():
        def body(c, _):
            rows = pl.ds(pl.multiple_of(c * rc, rc), rc)
            yn_ref[rows, :d_rg] = _rms_rows(rbuf[rows, :], gr_ref[...]).astype(BF16)
            yn_ref[rows, d_rg:] = _rms_rows(sbuf[rows, :], gs_ref[...]).astype(BF16)
            return 0

        lax.fori_loop(0, tm // rc, body, 0, unroll=2)

    _staged_rows([yr_hbm, ys_hbm], [rbuf, sbuf], [rsem, ssem], consume)
    o_ref[...] = h_ref[...] + jnp.dot(yn_ref[...], w_ref[...], preferred_element_type=F32)


def _outproj(y_rg, y_s5, g_rg, g_s5, w, h):
    m, d_rg = y_rg.shape
    d_s5 = y_s5.shape[1]
    d_mix, d = w.shape
    tm = _divisor_tile(m, 1024, 16)
    tn = _divisor_tile(d, min(512, d // 2), V7X_LANES)
    assert d // tn >= 2, (d, tn)
    vmem = tm * d_mix * 4 + tm * d_mix * 2 + 2 * d_mix * tn * 2 + 5 * tm * tn * 4
    return pl.pallas_call(
        _outproj_body,
        grid=(m // tm, d // tn),
        in_specs=[
            pl.BlockSpec(memory_space=pl.ANY),
            pl.BlockSpec(memory_space=pl.ANY),
            pl.BlockSpec((1, d_rg), lambda i, j: (0, 0)),
            pl.BlockSpec((1, d_s5), lambda i, j: (0, 0)),
            pl.BlockSpec((d_mix, tn), lambda i, j: (0, j)),
            pl.BlockSpec((tm, tn), lambda i, j: (i, j)),
        ],
        out_specs=pl.BlockSpec((tm, tn), lambda i, j: (i, j)),
        out_shape=jax.ShapeDtypeStruct((m, d), F32),
        scratch_shapes=[
            pltpu.VMEM((tm, d_mix), BF16),
            pltpu.VMEM((tm, d_rg), F32),
            pltpu.VMEM((tm, d_s5), F32),
            pltpu.SemaphoreType.DMA((1,)),
            pltpu.SemaphoreType.DMA((1,)),
        ],
        compiler_params=_params(("arbitrary", "arbitrary"), vmem),
        name="out_proj",
    )(y_rg, y_s5, g_rg.reshape(1, d_rg), g_s5.reshape(1, d_s5), w, h)


def _s5_tables(lam_re, lam_im, log_dt, b_re, b_im, c_re, c_im, slab):
    g, n = lam_re.shape
    c = b_re.shape[-1]
    gs = slab // c
    nslab = g // gs
    ell = S5_CHUNK
    dt = jnp.exp(log_dt.astype(F32))[:, None]
    lam = lax.complex(lam_re.astype(F32), lam_im.astype(F32))
    lam_dt = lam * dt
    lam_bar = jnp.exp(lam_dt)
    b_bar = ((lam_bar - 1.0) / lam)[..., None] * lax.complex(b_re.astype(F32), b_im.astype(F32))
    cc = lax.complex(c_re.astype(F32), c_im.astype(F32))

    def lam_pow(ks):
        return jnp.exp(lam_dt[None] * jnp.asarray(ks, F32)[:, None, None])

    def state_lanes(p):
        p = p.reshape(p.shape[0], nslab, gs * n)
        return jnp.concatenate([jnp.real(p), jnp.imag(p)], axis=-1).transpose(1, 0, 2)

    lam_tab = state_lanes(lam_pow([ell, 2 * ell, 4 * ell] + [ell] * 5))
    p8 = state_lanes(lam_pow([ell * (r + 1) for r in range(8)]))

    pw = lam_pow(list(range(ell + 1)))
    lag = np.arange(ell)[None, :] - np.arange(ell)[:, None]
    causal = jnp.asarray(lag >= 0, F32)[:, :, None, None]
    pw_lag = pw[np.clip(lag, 0, ell - 1)] * causal
    pw_lag = pw_lag.reshape(ell, ell, nslab, gs, n)
    cc_s = cc.reshape(nslab, gs, c, n)
    bb_s = b_bar.reshape(nslab, gs, n, c)
    k_m = jnp.real(jnp.einsum('sgdn,ijsgn,sgnc->sigcjd', cc_s, pw_lag, bb_s)).reshape(nslab, ell * gs * c, ell * c)
    wst = pw[ell - 1 - jnp.arange(ell)][..., None] * b_bar[None]
    wst = wst.reshape(ell, nslab, gs, n, c).transpose(1, 0, 2, 4, 3).reshape(nslab, ell * gs * c, n)
    w_m = jnp.concatenate([jnp.real(wst), jnp.imag(wst)], axis=-1)
    gg = cc[None] * pw[1:ell + 1][:, :, None, :]
    gg = gg.reshape(ell, nslab, gs, c, n).transpose(1, 2, 4, 0, 3).reshape(nslab, gs * n, ell * c)
    v_m = jnp.concatenate([jnp.real(gg), -jnp.imag(gg)], axis=1)
    return k_m, w_m, v_m, lam_tab, p8


def _s5_spread_consts(gs, c, n):
    ell = S5_CHUNK
    a1, q1 = np.arange(ell * c)[:, None], np.arange(ell * gs * c)[None, :]
    e1 = (a1 // c == q1 // (gs * c)) & (a1 % c == q1 % c)
    a2, q2 = np.arange(2 * n)[:, None], np.arange(2 * gs * n)[None, :]
    e2 = (a2 // n == q2 // (gs * n)) & (a2 % n == q2 % n)
    g_in = (np.arange(ell * gs * c) // c) % gs
    g_st = (np.arange(2 * gs * n) % (gs * n)) // n
    mask_m = g_in[:, None] == g_in[None, :]
    mask_w = g_in[:, None] == g_st[None, :]
    mask_v = g_st[:, None] == g_in[None, :]
    return tuple(jnp.asarray(m, dtype=BF16) for m in (e1, e2, mask_m, mask_w, mask_v))


def kernel(x, meta_tokens, ffn1_norm, ffn1_w_gate, ffn1_w_up, ffn1_w_down, mix_norm, w_in,
           rg_conv_w, rg_conv_b, rg_w_a, rg_b_a, rg_w_x, rg_b_x, rg_lambda,
           s5_lambda_re, s5_lambda_im, s5_log_dt, s5_b_re, s5_b_im, s5_c_re, s5_c_im, s5_d,
           s5_glu_w, s5_glu_b, rg_out_norm, s5_out_norm, w_out,
           ffn2_norm, ffn2_w_gate, ffn2_w_up, ffn2_w_down, final_norm):
    bsz, t_len, d = x.shape
    depth = ffn1_norm.shape[0]
    d_rg = rg_lambda.shape[-1]
    d_s5 = s5_d.shape[-1]
    heads, hd = rg_w_a.shape[1], rg_w_a.shape[2]
    slab = V7X_LANES
    s5_align = S5_CHUNK * V7X_SUBLANES

    h = x.reshape(bsz * t_len, d)
    hm = meta_tokens.astype(x.dtype)
    n_meta = hm.shape[0]
    for l in range(depth):
        last = l == depth - 1
        ffn1_w = (ffn1_w_gate[l].astype(BF16), ffn1_w_up[l].astype(BF16), ffn1_w_down[l].astype(BF16))
        ffn2_w = (ffn2_w_gate[l].astype(BF16), ffn2_w_up[l].astype(BF16), ffn2_w_down[l].astype(BF16))
        w_in_b = w_in[l].astype(BF16)
        w_out_b = w_out[l].astype(BF16)
        glu_w_b = s5_glu_w[l].astype(BF16)
        wax = jnp.concatenate([rg_w_a[l], rg_w_x[l]], axis=-1).astype(BF16)
        bax = jnp.concatenate([rg_b_a[l], rg_b_x[l]], axis=-1).reshape(heads, 1, 2 * hd)
        sp = jax.nn.softplus(-rg_lambda[l].astype(F32))
        s5_mats = _s5_tables(s5_lambda_re[l], s5_lambda_im[l], s5_log_dt[l], s5_b_re[l], s5_b_im[l],
                             s5_c_re[l], s5_c_im[l], slab)
        nslab, _, ns2 = s5_mats[3].shape
        s5_consts = _s5_spread_consts(slab // s5_b_re.shape[-1], s5_b_re.shape[-1], s5_lambda_re.shape[-1])
        fnorm = final_norm if last else ffn2_norm[l]

        def mixers(hrows, nb, h0, tail0, x0, front_pad):
            h1 = _ffn(hrows, ffn1_norm[l], *ffn1_w, ffn1_norm[l], final_norm=False)
            proj = _inproj(h1, mix_norm[l], w_in_b).reshape(nb, hrows.shape[0] // nb, -1)
            y_rg, h_t, tail_t = _rg_lru(proj, rg_conv_w[l], rg_conv_b[l], wax, bax, sp, h0, tail0, d_rg=d_rg)
            proj_s5 = jnp.pad(proj, ((0, 0), (front_pad, 0), (0, 0))) if front_pad else proj
            z, x_t = _s5_scan(proj_s5, s5_mats, s5_consts, s5_d[l], x0, col0=2 * d_rg, d_s5=d_s5)
            return h1, y_rg, z[:, front_pad:], h_t, tail_t, x_t

        def tail_layers(h1, y_rg, z):
            m = h1.shape[0]
            y_s5 = _glu(z.reshape(m, d_s5), glu_w_b, s5_glu_b[l])
            h2 = _outproj(y_rg.reshape(m, d_rg), y_s5, rg_out_norm[l], s5_out_norm[l], w_out_b, h1)
            return _ffn(h2, ffn2_norm[l], *ffn2_w, fnorm, final_norm=last)

        assert t_len % s5_align == 0, t_len
        zeros_rg = jnp.zeros((1, 8, d_rg), F32)
        zeros_s5 = jnp.zeros((1, nslab, 8, ns2), F32)
        meta_pad = (-n_meta) % s5_align
        h1m, y_rgm, zm, h_t, tail_t, x_t = mixers(hm, 1, zeros_rg, zeros_rg, zeros_s5, meta_pad)
        h1, y_rg, z, _, _, _ = mixers(h, bsz, h_t, tail_t, x_t, 0)
        h = tail_layers(h1, y_rg, z)
        if not last:
            hm = tail_layers(h1m, y_rgm, zm)
    return h.reshape(bsz, t_len, d)
```

```python
import functools
import math

import jax
import jax.numpy as jnp
import numpy as np
from jax import lax
from jax.experimental import pallas as pl
from jax.experimental.pallas import tpu as pltpu

EPS = 1e-6
RG_C = 8.0
F32 = jnp.float32
BF16 = jnp.bfloat16

V7X_LANES = 128
V7X_SUBLANES = 8
V7X_VMEM_BYTES = 64 * 1024 * 1024

S5_CHUNK = 8


def _divisor_tile(n, pref, align):
    if n <= pref:
        return n
    t = (pref // align) * align
    while t >= align:
        if n % t == 0:
            return t
        t -= align
    return n


def _params(sem, vmem_bytes):
    limit = int(min(V7X_VMEM_BYTES - (2 << 20), max(vmem_bytes + (6 << 20), 32 << 20)))
    return pltpu.CompilerParams(dimension_semantics=sem, vmem_limit_bytes=limit)


def _rms_rows(x, g):
    ms = jnp.mean(x * x, axis=-1, keepdims=True)
    return x * lax.rsqrt(ms + EPS) * g


def _sigmoid(x):
    return 1.0 / (1.0 + jnp.exp(-x))


def _gelu_tanh(x):
    c = math.sqrt(2.0 / math.pi)
    return 0.5 * x * (1.0 + jnp.tanh(c * (x + 0.044715 * (x * x * x))))


def _row_chunk(tm):
    return _divisor_tile(tm, 32, V7X_SUBLANES)


def _ffn_body(x_hbm, g_ref, wg_ref, wu_ref, wd_ref, g2_ref, o_ref, hn_ref, sem, *, n_split, final_norm, n_chunks):
    i = pl.program_id(0)
    j = pl.program_id(1)
    tm, d = o_ref.shape
    ck = tm // n_chunks
    rc = _row_chunk(ck)

    def x_copy(c):
        rows = pl.ds(pl.multiple_of(i * tm + c * ck, ck), ck)
        return pltpu.make_async_copy(x_hbm.at[rows, :], o_ref.at[pl.ds(c * ck, ck), :], sem.at[c])

    @pl.when(j == 0)
    def _():
        for c in range(n_chunks):
            x_copy(c).start()
        for c in range(n_chunks):
            x_copy(c).wait()

            def body(r, _):
                r0 = pl.multiple_of(c * ck + r * rc, rc)
                hn_ref[pl.ds(r0, rc), :] = _rms_rows(o_ref[pl.ds(r0, rc), :], g_ref[...]).astype(BF16)
                return 0

            lax.fori_loop(0, ck // rc, body, 0, unroll=2)

    g = jnp.dot(hn_ref[...], wg_ref[...], preferred_element_type=F32)
    u = jnp.dot(hn_ref[...], wu_ref[...], preferred_element_type=F32)
    a = (0.5 * (g * _sigmoid(g) * u)).astype(BF16)
    dn = d // n_split
    for s in range(n_split):
        o_ref[:, s * dn:(s + 1) * dn] += jnp.dot(a, wd_ref[:, s * dn:(s + 1) * dn].astype(BF16),
                                                 preferred_element_type=F32)

    if final_norm:
        @pl.when(j == pl.num_programs(1) - 1)
        def _():
            def body(c, _):
                r0 = pl.multiple_of(c * rc, rc)
                o_ref[pl.ds(r0, rc), :] = _rms_rows(o_ref[pl.ds(r0, rc), :], g2_ref[...])
                return 0

            lax.fori_loop(0, tm // rc, body, 0, unroll=2)


def _ffn(x, norm_w, wg, wu, wd, norm2_w, *, final_norm):
    m, d = x.shape
    f = wg.shape[1]
    tf = _divisor_tile(f, 256, V7X_LANES)
    tm = _divisor_tile(m, 1024, 16)
    n_chunks = max(1, tm // 128)
    n_split = max(1, d // 1024)
    vmem = 2 * tm * d * 4 + tm * d * 2 + 2 * d * tf * (2 * 2 + wd.dtype.itemsize) + tm * tf * 4 * 4
    body = functools.partial(_ffn_body, n_split=n_split, final_norm=final_norm, n_chunks=n_chunks)
    return pl.pallas_call(
        body,
        grid=(m // tm, f // tf),
        in_specs=[
            pl.BlockSpec(memory_space=pl.ANY),
            pl.BlockSpec((1, d), lambda i, j: (0, 0)),
            pl.BlockSpec((d, tf), lambda i, j: (0, j)),
            pl.BlockSpec((d, tf), lambda i, j: (0, j)),
            pl.BlockSpec((tf, d), lambda i, j: (j, 0)),
            pl.BlockSpec((1, d), lambda i, j: (0, 0)),
        ],
        out_specs=pl.BlockSpec((tm, d), lambda i, j: (i, 0)),
        out_shape=jax.ShapeDtypeStruct((m, d), F32),
        scratch_shapes=[pltpu.VMEM((tm, d), BF16), pltpu.SemaphoreType.DMA((n_chunks,))],
        compiler_params=_params(("parallel", "arbitrary"), vmem),
        name="ffn_swiglu",
    )(x, norm_w.reshape(1, d), wg, wu, wd, norm2_w.reshape(1, d))


def _staged_rows(srcs, bufs, sems, consume):
    i = pl.program_id(0)
    j = pl.program_id(1)
    tm = bufs[0].shape[0]

    def copies(tile):
        rows = pl.ds(pl.multiple_of(tile * tm, tm), tm)
        return [pltpu.make_async_copy(s.at[rows, :], b, m.at[0]) for s, b, m in zip(srcs, bufs, sems)]

    @pl.when(j == 0)
    def _():
        @pl.when(i == 0)
        def _():
            for cp in copies(0):
                cp.start()

        for cp in copies(i):
            cp.wait()
        consume()

    @pl.when(jnp.logical_and(j == 1, i + 1 < pl.num_programs(0)))
    def _():
        for cp in copies(i + 1):
            cp.start()


def _inproj_body(x_hbm, g_ref, w_ref, o_ref, hn_ref, xbuf, sem):
    tm = hn_ref.shape[0]
    rc = _row_chunk(tm)

    def consume():
        def body(c, _):
            r0 = pl.multiple_of(c * rc, rc)
            hn_ref[pl.ds(r0, rc), :] = _rms_rows(xbuf[pl.ds(r0, rc), :], g_ref[...]).astype(BF16)
            return 0

        lax.fori_loop(0, tm // rc, body, 0, unroll=2)

    _staged_rows([x_hbm], [xbuf], [sem], consume)
    o_ref[...] = jnp.dot(hn_ref[...], w_ref[...], preferred_element_type=F32)


def _inproj(x, norm_w, w):
    m, d = x.shape
    n = w.shape[1]
    tm = _divisor_tile(m, 1024, 16)
    tn = _divisor_tile(n, min(1024, n // 2), V7X_LANES)
    assert n // tn >= 2, (n, tn)
    vmem = tm * d * 4 + tm * d * 2 + 2 * d * tn * 2 + 3 * tm * tn * 4
    return pl.pallas_call(
        _inproj_body,
        grid=(m // tm, n // tn),
        in_specs=[
            pl.BlockSpec(memory_space=pl.ANY),
            pl.BlockSpec((1, d), lambda i, j: (0, 0)),
            pl.BlockSpec((d, tn), lambda i, j: (0, j)),
        ],
        out_specs=pl.BlockSpec((tm, tn), lambda i, j: (i, j)),
        out_shape=jax.ShapeDtypeStruct((m, n), F32),
        scratch_shapes=[pltpu.VMEM((tm, d), BF16), pltpu.VMEM((tm, d), F32), pltpu.SemaphoreType.DMA((1,))],
        compiler_params=_params(("arbitrary", "arbitrary"), vmem),
        name="in_proj",
    )(x, norm_w.reshape(1, d), w)


def _group_scan_real(a, b):
    row = lax.broadcasted_iota(jnp.int32, a.shape, 1)
    for dist in (1, 2, 4):
        keep = row >= dist
        a_prev = jnp.where(keep, pltpu.roll(a, dist, axis=1), 1.0)
        b_prev = jnp.where(keep, pltpu.roll(b, dist, axis=1), 0.0)
        b = a * b_prev + b
        a = a * a_prev
    return a, b


def _group_scan_const_complex(xr, xi, lam_ref, ns):
    row = lax.broadcasted_iota(jnp.int32, xr.shape, 1)
    for k, dist in enumerate((1, 2, 4)):
        lr = lam_ref[k:k + 1, :ns]
        li = lam_ref[k:k + 1, ns:]
        keep = row >= dist
        pr = jnp.where(keep, pltpu.roll(xr, dist, axis=1), 0.0)
        pi = jnp.where(keep, pltpu.roll(xi, dist, axis=1), 0.0)
        xr, xi = xr + (lr * pr - li * pi), xi + (lr * pi + li * pr)
    return xr, xi


def _rg_body(u_ref, gate_ref, cw_ref, cb_ref, wax_ref, bax_ref, sp_ref, h0_ref, tail0_ref,
             y_ref, ht_ref, tailt_ref, ext_ref, a_ref, b_ref, hc_ref, *, conv_width):
    t = pl.program_id(2)
    tc, width = u_ref.shape
    hpc, hd, _ = wax_ref.shape
    ng = tc // V7X_SUBLANES

    @pl.when(t == 0)
    def _():
        ext_ref[0:8, :] = tail0_ref[...]
        hc_ref[...] = h0_ref[...]

    ext_ref[8:8 + tc, :] = u_ref[...]
    for hh in range(hpc):
        cols = slice(hh * hd, (hh + 1) * hd)
        xc = cb_ref[:, cols] + cw_ref[conv_width - 1:conv_width, cols] * u_ref[:, cols]
        for k in range(conv_width - 1):
            back = conv_width - 1 - k
            xc = xc + cw_ref[k:k + 1, cols] * ext_ref[pl.ds(8 - back, tc), cols]
        pre = jnp.dot(xc.astype(BF16), wax_ref[hh], preferred_element_type=F32) + bax_ref[hh]
        r = _sigmoid(pre[:, :hd])
        i = _sigmoid(pre[:, hd:])
        a = jnp.exp((-RG_C) * r * sp_ref[:, cols])
        b = jnp.sqrt(1.0 - a * a) * i * xc
        a3, b3 = _group_scan_real(a.reshape(ng, 8, hd), b.reshape(ng, 8, hd))
        a_ref[:, cols] = a3.reshape(tc, hd)
        b_ref[:, cols] = b3.reshape(tc, hd)
    ext_ref[0:8, :] = ext_ref[tc:tc + 8, :]
    y_ref[...] = _gelu_tanh(gate_ref[...])

    def body(g, hc):
        r0 = pl.multiple_of(g * 8, 8)
        h = a_ref[pl.ds(r0, 8), :] * hc + b_ref[pl.ds(r0, 8), :]
        y_ref[pl.ds(r0, 8), :] = h * y_ref[pl.ds(r0, 8), :]
        return jnp.broadcast_to(h[7:8, :], (8, width))

    hc = lax.fori_loop(0, ng, body, hc_ref[...], unroll=2)
    hc_ref[...] = hc

    @pl.when(t == pl.num_programs(2) - 1)
    def _():
        ht_ref[...] = hc
        tailt_ref[...] = ext_ref[0:8, :]


def _rg_lru(proj, conv_w, conv_b, wax, bax, softplus_neg_lam, h0, tail0, *, d_rg):
    bsz, t_len, _ = proj.shape
    heads, hd, _ = wax.shape
    conv_width = conv_w.shape[0]
    tc = _divisor_tile(t_len, 512, V7X_SUBLANES)
    hpc = _divisor_tile(heads, max(1, 1024 // hd), 1)
    wd = hpc * hd
    ncell = heads // hpc
    vmem = 2 * 3 * tc * wd * 4 + (tc + 8) * wd * 4 + 2 * tc * wd * 4 + 2 * hpc * hd * 2 * hd * 2 + 8 * tc * hd * 4
    body = functools.partial(_rg_body, conv_width=conv_width)
    hsel = lambda b, h, t: (0, 0, h)
    return pl.pallas_call(
        body,
        grid=(bsz, ncell, t_len // tc),
        in_specs=[
            pl.BlockSpec((None, tc, wd), lambda b, h, t: (b, t, h)),
            pl.BlockSpec((None, tc, wd), lambda b, h, t: (b, t, ncell + h)),
            pl.BlockSpec((conv_width, wd), lambda b, h, t: (0, h)),
            pl.BlockSpec((1, wd), lambda b, h, t: (0, h)),
            pl.BlockSpec((hpc, hd, 2 * hd), lambda b, h, t: (h, 0, 0)),
            pl.BlockSpec((hpc, 1, 2 * hd), lambda b, h, t: (h, 0, 0)),
            pl.BlockSpec((1, wd), lambda b, h, t: (0, h)),
            pl.BlockSpec((None, 8, wd), hsel),
            pl.BlockSpec((None, 8, wd), hsel),
        ],
        out_specs=[
            pl.BlockSpec((None, tc, wd), lambda b, h, t: (b, t, h)),
            pl.BlockSpec((None, 8, wd), lambda b, h, t: (b, 0, h)),
            pl.BlockSpec((None, 8, wd), lambda b, h, t: (b, 0, h)),
        ],
        out_shape=[
            jax.ShapeDtypeStruct((bsz, t_len, d_rg), F32),
            jax.ShapeDtypeStruct((bsz, 8, d_rg), F32),
            jax.ShapeDtypeStruct((bsz, 8, d_rg), F32),
        ],
        scratch_shapes=[
            pltpu.VMEM((tc + 8, wd), F32),
            pltpu.VMEM((tc, wd), F32),
            pltpu.VMEM((tc, wd), F32),
            pltpu.VMEM((8, wd), F32),
        ],
        compiler_params=_params(("parallel", "parallel", "arbitrary"), vmem),
        name="rg_lru",
    )(proj, proj, conv_w, conv_b.reshape(1, d_rg), wax, bax, softplus_neg_lam.reshape(1, d_rg),
      h0, tail0)


def _s5_body(u_ref, km_ref, wsm_ref, vm_ref, e1_ref, e2_ref, mm_ref, mw_ref, mv_ref, lam_ref, p8_ref, d_ref,
             x0_ref, z_ref, xt_ref, mi_ref, ws_ref, v_ref, s_ref, xc_ref):
    t = pl.program_id(2)
    tc, slab = u_ref.shape
    nc = tc // S5_CHUNK
    ns = lam_ref.shape[1] // 2
    ng = nc // V7X_SUBLANES

    @pl.when(jnp.logical_and(pl.program_id(1) == 0, t == 0))
    def _():
        def expand(small_ref, e_ref, mask_ref, dst_ref):
            width = dst_ref.shape[1]
            piece = min(width, 2 * V7X_LANES)
            small = small_ref[...].astype(BF16)
            for q0 in range(0, width, piece):
                full = jnp.dot(small, e_ref[:, q0:q0 + piece], preferred_element_type=F32)
                dst_ref[:, q0:q0 + piece] = (full * mask_ref[:, q0:q0 + piece].astype(F32)).astype(BF16)

        expand(km_ref, e1_ref, mm_ref, mi_ref)
        expand(wsm_ref, e2_ref, mw_ref, ws_ref)
        expand(vm_ref, e1_ref, mv_ref, v_ref)

    @pl.when(t == 0)
    def _():
        xc_ref[...] = x0_ref[...]

    us = [u_ref[pl.ds(i, nc, stride=S5_CHUNK), :] for i in range(S5_CHUNK)]
    ucat = jnp.concatenate([ui.astype(BF16) for ui in us], axis=-1)
    send = jnp.dot(ucat, ws_ref[...], preferred_element_type=F32)
    ycat = jnp.dot(ucat, mi_ref[...], preferred_element_type=F32)

    xr, xi = _group_scan_const_complex(send[:, :ns].reshape(ng, 8, ns), send[:, ns:].reshape(ng, 8, ns),
                                       lam_ref, ns)
    s_ref[7:8, :] = xc_ref[0:1, :]
    s_ref[8:8 + nc, :ns] = xr.reshape(nc, ns)
    s_ref[8:8 + nc, ns:] = xi.reshape(nc, ns)

    p8r = p8_ref[:, :ns]
    p8i = p8_ref[:, ns:]

    def body(g, carry):
        cr, ci = carry
        r0 = pl.multiple_of(8 + g * 8, 8)
        nr = s_ref[pl.ds(r0, 8), :ns] + (p8r * cr - p8i * ci)
        ni = s_ref[pl.ds(r0, 8), ns:] + (p8r * ci + p8i * cr)
        s_ref[pl.ds(r0, 8), :ns] = nr
        s_ref[pl.ds(r0, 8), ns:] = ni
        return (jnp.broadcast_to(nr[7:8, :], (8, ns)), jnp.broadcast_to(ni[7:8, :], (8, ns)))

    cr, ci = lax.fori_loop(0, ng, body, (xc_ref[:, :ns], xc_ref[:, ns:]), unroll=True)
    xc_ref[:, :ns] = cr
    xc_ref[:, ns:] = ci

    x_in = s_ref[pl.ds(7, nc), :]
    ycat = ycat + jnp.dot(x_in.astype(BF16), v_ref[...], preferred_element_type=F32)
    for i in range(S5_CHUNK):
        y = ycat[:, i * slab:(i + 1) * slab] + d_ref[...] * us[i]
        z_ref[pl.ds(i, nc, stride=S5_CHUNK), :] = _gelu_tanh(y)

    @pl.when(t == pl.num_programs(2) - 1)
    def _():
        xt_ref[...] = xc_ref[...]


def _s5_scan(proj, tables, consts, d_skip, x0, *, col0, d_s5):
    bsz, t_len, _ = proj.shape
    k_m, w_m, v_m, lam_pow, p8 = tables
    e1, e2, mask_m, mask_w, mask_v = consts
    nslab, kdim, _ = k_m.shape
    slab = kdim // S5_CHUNK
    ns2 = lam_pow.shape[-1]
    tc = _divisor_tile(t_len, 4096, S5_CHUNK * V7X_SUBLANES)
    assert tc % (S5_CHUNK * V7X_SUBLANES) == 0, (t_len, tc)
    nc = tc // S5_CHUNK
    cb0 = col0 // slab
    big = 2 * kdim * kdim + kdim * ns2
    vmem = (4 * tc * slab * 4 + 2 * big * 2 + big * 2 + 2 * (2 * kdim + ns2) * slab * 4 + (nc + 8) * ns2 * 4
            + nc * (2 * kdim * 4 + kdim * 2 + 3 * ns2 * 4))
    slab_blk = lambda a: pl.BlockSpec((None,) + a.shape[1:], lambda s, b, t: (s, 0, 0))
    const_blk = lambda a: pl.BlockSpec(a.shape, lambda s, b, t: (0, 0))
    return pl.pallas_call(
        _s5_body,
        grid=(nslab, bsz, t_len // tc),
        in_specs=[
            pl.BlockSpec((None, tc, slab), lambda s, b, t: (b, t, cb0 + s)),
            slab_blk(k_m), slab_blk(w_m), slab_blk(v_m),
            const_blk(e1), const_blk(e2), const_blk(mask_m), const_blk(mask_w), const_blk(mask_v),
            pl.BlockSpec((None, 8, ns2), lambda s, b, t: (s, 0, 0)),
            pl.BlockSpec((None, 8, ns2), lambda s, b, t: (s, 0, 0)),
            pl.BlockSpec((1, slab), lambda s, b, t: (0, s)),
            pl.BlockSpec((None, None, 8, ns2), lambda s, b, t: (0, s, 0, 0)),
        ],
        out_specs=[
            pl.BlockSpec((None, tc, slab), lambda s, b, t: (b, t, s)),
            pl.BlockSpec((None, None, 8, ns2), lambda s, b, t: (b, s, 0, 0)),
        ],
        out_shape=[
            jax.ShapeDtypeStruct((bsz, t_len, d_s5), F32),
            jax.ShapeDtypeStruct((bsz, nslab, 8, ns2), F32),
        ],
        scratch_shapes=[
            pltpu.VMEM((kdim, kdim), BF16),
            pltpu.VMEM((kdim, ns2), BF16),
            pltpu.VMEM((ns2, kdim), BF16),
            pltpu.VMEM((nc + 8, ns2), F32),
            pltpu.VMEM((8, ns2), F32),
        ],
        compiler_params=_params(("parallel", "arbitrary", "arbitrary"), vmem),
        name="s5_scan",
    )(proj, k_m, w_m, v_m, e1, e2, mask_m, mask_w, mask_v, lam_pow, p8, d_skip.reshape(1, d_s5), x0)


def _glu_body(z_ref, w_ref, b_ref, o_ref, zb_ref):
    tm = z_ref.shape[0]
    tn = o_ref.shape[1]
    rc = _row_chunk(tm)

    @pl.when(pl.program_id(1) == 0)
    def _():
        def body(c, _):
            r0 = pl.multiple_of(c * rc, rc)
            zb_ref[pl.ds(r0, rc), :] = z_ref[pl.ds(r0, rc), :].astype(BF16)
            return 0

        lax.fori_loop(0, tm // rc, body, 0, unroll=2)

    pre = jnp.dot(zb_ref[...], w_ref[...], preferred_element_type=F32) + b_ref[...]
    cols = pl.ds(pl.multiple_of(pl.program_id(1) * tn, tn), tn)
    o_ref[...] = z_ref[:, cols] * _sigmoid(pre)


def _glu(z, w, b):
    m, d = z.shape
    tm = _divisor_tile(m, 1024, 16)
    tn = _divisor_tile(d, 1024, V7X_LANES)
    vmem = 2 * tm * d * 4 + tm * d * 2 + 2 * d * tn * 2 + 4 * tm * tn * 4
    return pl.pallas_call(
        _glu_body,
        grid=(m // tm, d // tn),
        in_specs=[
            pl.BlockSpec((tm, d), lambda i, j: (i, 0)),
            pl.BlockSpec((d, tn), lambda i, j: (0, j)),
            pl.BlockSpec((1, tn), lambda i, j: (0, j)),
        ],
        out_specs=pl.BlockSpec((tm, tn), lambda i, j: (i, j)),
        out_shape=jax.ShapeDtypeStruct((m, d), F32),
        scratch_shapes=[pltpu.VMEM((tm, d), BF16)],
        compiler_params=_params(("parallel", "arbitrary"), vmem),
        name="s5_glu",
    )(z, w, b.reshape(1, d))


def _outproj_body(yr_hbm, ys_hbm, gr_ref, gs_ref, w_ref, h_ref, o_ref, yn_ref, rbuf, sbuf, rsem, ssem):
    tm = yn_ref.shape[0]
    d_rg = rbuf.shape[-1]
    rc = _row_chunk(tm)

    def consume():
        def body(c, _):
            rows = pl.ds(pl.multiple_of(c * rc, rc), rc)
            yn_ref[rows, :d_rg] = _rms_rows(rbuf[rows, :], gr_ref[...]).astype(BF16)
            yn_ref[rows, d_rg:] = _rms_rows(sbuf[rows, :], gs_ref[...]).astype(BF16)
            return 0

        lax.fori_loop(0, tm // rc, body, 0, unroll=2)

    _staged_rows([yr_hbm, ys_hbm], [rbuf, sbuf], [rsem, ssem], consume)
    o_ref[...] = h_ref[...] + jnp.dot(yn_ref[...], w_ref[...], preferred_element_type=F32)


def _outproj(y_rg, y_s5, g_rg, g_s5, w, h):
    m, d_rg = y_rg.shape
    d_s5 = y_s5.shape[1]
    d_mix, d = w.shape
    tm = _divisor_tile(m, 1024, 16)
    tn = _divisor_tile(d, min(512, d // 2), V7X_LANES)
    assert d // tn >= 2, (d, tn)
    vmem = tm * d_mix * 4 + tm * d_mix * 2 + 2 * d_mix * tn * 2 + 5 * tm * tn * 4
    return pl.pallas_call(
        _outproj_body,
        grid=(m // tm, d // tn),
        in_specs=[
            pl.BlockSpec(memory_space=pl.ANY),
            pl.BlockSpec(memory_space=pl.ANY),
            pl.BlockSpec((1, d_rg), lambda i, j: (0, 0)),
            pl.BlockSpec((1, d_s5), lambda i, j: (0, 0)),
            pl.BlockSpec((d_mix, tn), lambda i, j: (0, j)),
            pl.BlockSpec((tm, tn), lambda i, j: (i, j)),
        ],
        out_specs=pl.BlockSpec((tm, tn), lambda i, j: (i, j)),
        out_shape=jax.ShapeDtypeStruct((m, d), F32),
        scratch_shapes=[
            pltpu.VMEM((tm, d_mix), BF16),
            pltpu.VMEM((tm, d_rg), F32),
            pltpu.VMEM((tm, d_s5), F32),
            pltpu.SemaphoreType.DMA((1,)),
            pltpu.SemaphoreType.DMA((1,)),
        ],
        compiler_params=_params(("arbitrary", "arbitrary"), vmem),
        name="out_proj",
    )(y_rg, y_s5, g_rg.reshape(1, d_rg), g_s5.reshape(1, d_s5), w, h)


def _s5_tables(lam_re, lam_im, log_dt, b_re, b_im, c_re, c_im, slab):
    g, n = lam_re.shape
    c = b_re.shape[-1]
    gs = slab // c
    nslab = g // gs
    ell = S5_CHUNK
    dt = jnp.exp(log_dt.astype(F32))[:, None]
    lam = lax.complex(lam_re.astype(F32), lam_im.astype(F32))
    lam_dt = lam * dt
    lam_bar = jnp.exp(lam_dt)
    b_bar = ((lam_bar - 1.0) / lam)[..., None] * lax.complex(b_re.astype(F32), b_im.astype(F32))
    cc = lax.complex(c_re.astype(F32), c_im.astype(F32))

    def lam_pow(ks):
        return jnp.exp(lam_dt[None] * jnp.asarray(ks, F32)[:, None, None])

    def state_lanes(p):
        p = p.reshape(p.shape[0], nslab, gs * n)
        return jnp.concatenate([jnp.real(p), jnp.imag(p)], axis=-1).transpose(1, 0, 2)

    lam_tab = state_lanes(lam_pow([ell, 2 * ell, 4 * ell] + [ell] * 5))
    p8 = state_lanes(lam_pow([ell * (r + 1) for r in range(8)]))

    pw = lam_pow(list(range(ell + 1)))
    exact = lax.Precision.HIGHEST

    def slab_major(a):
        return a.reshape(ell, nslab, gs * c, a.shape[-1]).transpose(1, 0, 2, 3).reshape(nslab, ell * gs * c, a.shape[-1])

    kk = jnp.real(jnp.einsum('gdn,kgn,gnc->gckd', cc, pw[:ell], b_bar)).reshape(g * c, ell * c)
    place = np.zeros((ell, ell * c, ell * c), np.float32)
    for i in range(ell):
        for k in range(ell - i):
            place[i, k * c + np.arange(c), (i + k) * c + np.arange(c)] = 1.0
    k_m = slab_major(jnp.einsum('rk,ikq->irq', kk, jnp.asarray(place), precision=exact))
    p_st = pw[ell - 1 - np.arange(ell)][:, :, None, :]
    b_t = b_bar.transpose(0, 2, 1)[None]
    pr, pi, br, bi = jnp.real(p_st), jnp.imag(p_st), jnp.real(b_t), jnp.imag(b_t)
    w_m = slab_major(jnp.concatenate([pr * br - pi * bi, pr * bi + pi * br], axis=-1).reshape(ell, g * c, 2 * n))
    rep_i = np.kron(np.eye(ell, dtype=np.float32), np.ones((1, c), np.float32))
    rep_c = np.kron(np.ones((1, ell), np.float32), np.eye(c, dtype=np.float32))
    p_out = pw[1:ell + 1].transpose(1, 2, 0).reshape(g * n, ell)
    c_t = cc.transpose(0, 2, 1).reshape(g * n, c)
    ar, ai = (jnp.dot(part(p_out), rep_i, precision=exact) for part in (jnp.real, jnp.imag))
    cr, ci = (jnp.dot(part(c_t), rep_c, precision=exact) for part in (jnp.real, jnp.imag))
    v_m = jnp.concatenate([(cr * ar - ci * ai).reshape(nslab, gs * n, ell * c),
                           -(cr * ai + ci * ar).reshape(nslab, gs * n, ell * c)], axis=1)
    return k_m, w_m, v_m, lam_tab, p8


def _s5_spread_consts(gs, c, n):
    ell = S5_CHUNK
    a1, q1 = np.arange(ell * c)[:, None], np.arange(ell * gs * c)[None, :]
    e1 = (a1 // c == q1 // (gs * c)) & (a1 % c == q1 % c)
    a2, q2 = np.arange(2 * n)[:, None], np.arange(2 * gs * n)[None, :]
    e2 = (a2 // n == q2 // (gs * n)) & (a2 % n == q2 % n)
    g_in = (np.arange(ell * gs * c) // c) % gs
    g_st = (np.arange(2 * gs * n) % (gs * n)) // n
    mask_m = g_in[:, None] == g_in[None, :]
    mask_w = g_in[:, None] == g_st[None, :]
    mask_v = g_st[:, None] == g_in[None, :]
    return tuple(jnp.asarray(m, dtype=BF16) for m in (e1, e2, mask_m, mask_w, mask_v))


def kernel(x, meta_tokens, ffn1_norm, ffn1_w_gate, ffn1_w_up, ffn1_w_down, mix_norm, w_in,
           rg_conv_w, rg_conv_b, rg_w_a, rg_b_a, rg_w_x, rg_b_x, rg_lambda,
           s5_lambda_re, s5_lambda_im, s5_log_dt, s5_b_re, s5_b_im, s5_c_re, s5_c_im, s5_d,
           s5_glu_w, s5_glu_b, rg_out_norm, s5_out_norm, w_out,
           ffn2_norm, ffn2_w_gate, ffn2_w_up, ffn2_w_down, final_norm):
    bsz, t_len, d = x.shape
    depth = ffn1_norm.shape[0]
    d_rg = rg_lambda.shape[-1]
    d_s5 = s5_d.shape[-1]
    heads, hd = rg_w_a.shape[1], rg_w_a.shape[2]
    slab = V7X_LANES
    s5_align = S5_CHUNK * V7X_SUBLANES

    h = x.reshape(bsz * t_len, d)
    hm = meta_tokens.astype(x.dtype)
    n_meta = hm.shape[0]
    for l in range(depth):
        last = l == depth - 1
        ffn1_w = (ffn1_w_gate[l].astype(BF16), ffn1_w_up[l].astype(BF16), ffn1_w_down[l].astype(BF16))
        ffn2_w = (ffn2_w_gate[l].astype(BF16), ffn2_w_up[l].astype(BF16), ffn2_w_down[l].astype(BF16))
        w_in_b = w_in[l].astype(BF16)
        w_out_b = w_out[l].astype(BF16)
        glu_w_b = s5_glu_w[l].astype(BF16)
        wax = jnp.concatenate([rg_w_a[l], rg_w_x[l]], axis=-1).astype(BF16)
        bax = jnp.concatenate([rg_b_a[l], rg_b_x[l]], axis=-1).reshape(heads, 1, 2 * hd)
        sp = jax.nn.softplus(-rg_lambda[l].astype(F32))
        s5_mats = _s5_tables(s5_lambda_re[l], s5_lambda_im[l], s5_log_dt[l], s5_b_re[l], s5_b_im[l],
                             s5_c_re[l], s5_c_im[l], slab)
        nslab, _, ns2 = s5_mats[3].shape
        s5_consts = _s5_spread_consts(slab // s5_b_re.shape[-1], s5_b_re.shape[-1], s5_lambda_re.shape[-1])
        fnorm = final_norm if last else ffn2_norm[l]

        def mixers(hrows, nb, h0, tail0, x0, front_pad):
            h1 = _ffn(hrows, ffn1_norm[l], *ffn1_w, ffn1_norm[l], final_norm=False)
            proj = _inproj(h1, mix_norm[l], w_in_b).reshape(nb, hrows.shape[0] // nb, -1)
            y_rg, h_t, tail_t = _rg_lru(proj, rg_conv_w[l], rg_conv_b[l], wax, bax, sp, h0, tail0, d_rg=d_rg)
            proj_s5 = jnp.pad(proj, ((0, 0), (front_pad, 0), (0, 0))) if front_pad else proj
            z, x_t = _s5_scan(proj_s5, s5_mats, s5_consts, s5_d[l], x0, col0=2 * d_rg, d_s5=d_s5)
            return h1, y_rg, z[:, front_pad:], h_t, tail_t, x_t

        def tail_layers(h1, y_rg, z):
            m = h1.shape[0]
            y_s5 = _glu(z.reshape(m, d_s5), glu_w_b, s5_glu_b[l])
            h2 = _outproj(y_rg.reshape(m, d_rg), y_s5, rg_out_norm[l], s5_out_norm[l], w_out_b, h1)
            return _ffn(h2, ffn2_norm[l], *ffn2_w, fnorm, final_norm=last)

        assert t_len % s5_align == 0, t_len
        zeros_rg = jnp.zeros((1, 8, d_rg), F32)
        zeros_s5 = jnp.zeros((1, nslab, 8, ns2), F32)
        meta_pad = (-n_meta) % s5_align
        h1m, y_rgm, zm, h_t, tail_t, x_t = mixers(hm, 1, zeros_rg, zeros_rg, zeros_s5, meta_pad)
        h1, y_rg, z, _, _, _ = mixers(h, bsz, h_t, tail_t, x_t, 0)
        h = tail_layers(h1, y_rg, z)
        if not last:
            hm = tail_layers(h1m, y_rgm, zm)
    return h.reshape(bsz, t_len, d)
```

```python
import functools
import math

import jax
import jax.numpy as jnp
import numpy as np
from jax import lax
from jax.experimental import pallas as pl
from jax.experimental.pallas import tpu as pltpu

EPS = 1e-6
RG_C = 8.0
F32 = jnp.float32
BF16 = jnp.bfloat16

V7X_LANES = 128
V7X_SUBLANES = 8
V7X_VMEM_BYTES = 64 * 1024 * 1024

S5_CHUNK = 8


def _divisor_tile(n, pref, align):
    if n <= pref:
        return n
    t = (pref // align) * align
    while t >= align:
        if n % t == 0:
            return t
        t -= align
    return n


def _params(sem, vmem_bytes):
    limit = int(min(V7X_VMEM_BYTES - (2 << 20), max(vmem_bytes + (6 << 20), 32 << 20)))
    return pltpu.CompilerParams(dimension_semantics=sem, vmem_limit_bytes=limit)


def _rms_rows(x, g):
    ms = jnp.mean(x * x, axis=-1, keepdims=True)
    return x * lax.rsqrt(ms + EPS) * g


def _sigmoid(x):
    return 1.0 / (1.0 + jnp.exp(-x))


def _gelu_tanh(x):
    c = math.sqrt(2.0 / math.pi)
    return 0.5 * x * (1.0 + jnp.tanh(c * (x + 0.044715 * (x * x * x))))


def _row_chunk(tm):
    return _divisor_tile(tm, 32, V7X_SUBLANES)


def _ffn_body(x_hbm, g_ref, wg_ref, wu_ref, wd_ref, g2_ref, o_ref, hn_ref, sem, *, n_split, final_norm, n_chunks):
    i = pl.program_id(0)
    j = pl.program_id(1)
    tm, d = o_ref.shape
    ck = tm // n_chunks
    rc = _row_chunk(ck)

    def x_copy(c):
        rows = pl.ds(pl.multiple_of(i * tm + c * ck, ck), ck)
        return pltpu.make_async_copy(x_hbm.at[rows, :], o_ref.at[pl.ds(c * ck, ck), :], sem.at[c])

    @pl.when(j == 0)
    def _():
        for c in range(n_chunks):
            x_copy(c).start()
        for c in range(n_chunks):
            x_copy(c).wait()

            def body(r, _):
                r0 = pl.multiple_of(c * ck + r * rc, rc)
                hn_ref[pl.ds(r0, rc), :] = _rms_rows(o_ref[pl.ds(r0, rc), :], g_ref[...]).astype(BF16)
                return 0

            lax.fori_loop(0, ck // rc, body, 0, unroll=2)

    g = jnp.dot(hn_ref[...], wg_ref[...], preferred_element_type=F32)
    u = jnp.dot(hn_ref[...], wu_ref[...], preferred_element_type=F32)
    a = (0.5 * (g * _sigmoid(g) * u)).astype(BF16)
    dn = d // n_split
    for s in range(n_split):
        o_ref[:, s * dn:(s + 1) * dn] += jnp.dot(a, wd_ref[:, s * dn:(s + 1) * dn].astype(BF16),
                                                 preferred_element_type=F32)

    if final_norm:
        @pl.when(j == pl.num_programs(1) - 1)
        def _():
            def body(c, _):
                r0 = pl.multiple_of(c * rc, rc)
                o_ref[pl.ds(r0, rc), :] = _rms_rows(o_ref[pl.ds(r0, rc), :], g2_ref[...])
                return 0

            lax.fori_loop(0, tm // rc, body, 0, unroll=2)


def _ffn(x, norm_w, wg, wu, wd, norm2_w, *, final_norm):
    m, d = x.shape
    f = wg.shape[1]
    tf = _divisor_tile(f, 256, V7X_LANES)
    tm = _divisor_tile(m, 1024, 16)
    n_chunks = max(1, tm // 128)
    n_split = max(1, d // 1024)
    vmem = 2 * tm * d * 4 + tm * d * 2 + 2 * d * tf * (2 * 2 + wd.dtype.itemsize) + tm * tf * 4 * 4
    body = functools.partial(_ffn_body, n_split=n_split, final_norm=final_norm, n_chunks=n_chunks)
    return pl.pallas_call(
        body,
        grid=(m // tm, f // tf),
        in_specs=[
            pl.BlockSpec(memory_space=pl.ANY),
            pl.BlockSpec((1, d), lambda i, j: (0, 0)),
            pl.BlockSpec((d, tf), lambda i, j: (0, j)),
            pl.BlockSpec((d, tf), lambda i, j: (0, j)),
            pl.BlockSpec((tf, d), lambda i, j: (j, 0)),
            pl.BlockSpec((1, d), lambda i, j: (0, 0)),
        ],
        out_specs=pl.BlockSpec((tm, d), lambda i, j: (i, 0)),
        out_shape=jax.ShapeDtypeStruct((m, d), F32),
        scratch_shapes=[pltpu.VMEM((tm, d), BF16), pltpu.SemaphoreType.DMA((n_chunks,))],
        compiler_params=_params(("parallel", "arbitrary"), vmem),
        name="ffn_swiglu",
    )(x, norm_w.reshape(1, d), wg, wu, wd, norm2_w.reshape(1, d))


def _staged_rows(srcs, bufs, sems, consume):
    i = pl.program_id(0)
    j = pl.program_id(1)
    tm = bufs[0].shape[0]

    def copies(tile):
        rows = pl.ds(pl.multiple_of(tile * tm, tm), tm)
        return [pltpu.make_async_copy(s.at[rows, :], b, m.at[0]) for s, b, m in zip(srcs, bufs, sems)]

    @pl.when(j == 0)
    def _():
        @pl.when(i == 0)
        def _():
            for cp in copies(0):
                cp.start()

        for cp in copies(i):
            cp.wait()
        consume()

    @pl.when(jnp.logical_and(j == 1, i + 1 < pl.num_programs(0)))
    def _():
        for cp in copies(i + 1):
            cp.start()


def _inproj_body(x_hbm, g_ref, w_ref, o_ref, hn_ref, xbuf, sem):
    tm = hn_ref.shape[0]
    rc = _row_chunk(tm)

    def consume():
        def body(c, _):
            r0 = pl.multiple_of(c * rc, rc)
            hn_ref[pl.ds(r0, rc), :] = _rms_rows(xbuf[pl.ds(r0, rc), :], g_ref[...]).astype(BF16)
            return 0

        lax.fori_loop(0, tm // rc, body, 0, unroll=2)

    _staged_rows([x_hbm], [xbuf], [sem], consume)
    o_ref[...] = jnp.dot(hn_ref[...], w_ref[...], preferred_element_type=F32)


def _inproj(x, norm_w, w):
    m, d = x.shape
    n = w.shape[1]
    tm = _divisor_tile(m, 1024, 16)
    tn = _divisor_tile(n, min(1024, n // 2), V7X_LANES)
    assert n // tn >= 2, (n, tn)
    vmem = tm * d * 4 + tm * d * 2 + 2 * d * tn * 2 + 3 * tm * tn * 4
    return pl.pallas_call(
        _inproj_body,
        grid=(m // tm, n // tn),
        in_specs=[
            pl.BlockSpec(memory_space=pl.ANY),
            pl.BlockSpec((1, d), lambda i, j: (0, 0)),
            pl.BlockSpec((d, tn), lambda i, j: (0, j)),
        ],
        out_specs=pl.BlockSpec((tm, tn), lambda i, j: (i, j)),
        out_shape=jax.ShapeDtypeStruct((m, n), F32),
        scratch_shapes=[pltpu.VMEM((tm, d), BF16), pltpu.VMEM((tm, d), F32), pltpu.SemaphoreType.DMA((1,))],
        compiler_params=_params(("arbitrary", "arbitrary"), vmem),
        name="in_proj",
    )(x, norm_w.reshape(1, d), w)


def _group_scan_real(a, b):
    row = lax.broadcasted_iota(jnp.int32, a.shape, 1)
    for dist in (1, 2, 4):
        keep = row >= dist
        a_prev = jnp.where(keep, pltpu.roll(a, dist, axis=1), 1.0)
        b_prev = jnp.where(keep, pltpu.roll(b, dist, axis=1), 0.0)
        b = a * b_prev + b
        a = a * a_prev
    return a, b


def _group_scan_const_complex(xr, xi, lam_ref, ns):
    row = lax.broadcasted_iota(jnp.int32, xr.shape, 1)
    for k, dist in enumerate((1, 2, 4)):
        lr = lam_ref[k:k + 1, :ns]
        li = lam_ref[k:k + 1, ns:]
        keep = row >= dist
        pr = jnp.where(keep, pltpu.roll(xr, dist, axis=1), 0.0)
        pi = jnp.where(keep, pltpu.roll(xi, dist, axis=1), 0.0)
        xr, xi = xr + (lr * pr - li * pi), xi + (lr * pi + li * pr)
    return xr, xi


def _rg_body(u_ref, gate_ref, cw_ref, cb_ref, wax_ref, bax_ref, sp_ref, h0_ref, tail0_ref,
             y_ref, ht_ref, tailt_ref, ext_ref, a_ref, b_ref, hc_ref, *, conv_width):
    t = pl.program_id(2)
    tc, width = u_ref.shape
    hpc, hd, _ = wax_ref.shape
    ng = tc // V7X_SUBLANES

    @pl.when(t == 0)
    def _():
        ext_ref[0:8, :] = tail0_ref[...]
        hc_ref[...] = h0_ref[...]

    ext_ref[8:8 + tc, :] = u_ref[...]
    for hh in range(hpc):
        cols = slice(hh * hd, (hh + 1) * hd)
        xc = cb_ref[:, cols] + cw_ref[conv_width - 1:conv_width, cols] * u_ref[:, cols]
        for k in range(conv_width - 1):
            back = conv_width - 1 - k
            xc = xc + cw_ref[k:k + 1, cols] * ext_ref[pl.ds(8 - back, tc), cols]
        pre = jnp.dot(xc.astype(BF16), wax_ref[hh], preferred_element_type=F32) + bax_ref[hh]
        r = _sigmoid(pre[:, :hd])
        i = _sigmoid(pre[:, hd:])
        a = jnp.exp((-RG_C) * r * sp_ref[:, cols])
        b = jnp.sqrt(1.0 - a * a) * i * xc
        a3, b3 = _group_scan_real(a.reshape(ng, 8, hd), b.reshape(ng, 8, hd))
        a_ref[:, cols] = a3.reshape(tc, hd)
        b_ref[:, cols] = b3.reshape(tc, hd)
    ext_ref[0:8, :] = ext_ref[tc:tc + 8, :]
    y_ref[...] = _gelu_tanh(gate_ref[...])

    def body(g, hc):
        r0 = pl.multiple_of(g * 8, 8)
        h = a_ref[pl.ds(r0, 8), :] * hc + b_ref[pl.ds(r0, 8), :]
        y_ref[pl.ds(r0, 8), :] = h * y_ref[pl.ds(r0, 8), :]
        return jnp.broadcast_to(h[7:8, :], (8, width))

    hc = lax.fori_loop(0, ng, body, hc_ref[...], unroll=2)
    hc_ref[...] = hc

    @pl.when(t == pl.num_programs(2) - 1)
    def _():
        ht_ref[...] = hc
        tailt_ref[...] = ext_ref[0:8, :]


def _rg_lru(proj, conv_w, conv_b, wax, bax, softplus_neg_lam, h0, tail0, *, d_rg):
    bsz, t_len, _ = proj.shape
    heads, hd, _ = wax.shape
    conv_width = conv_w.shape[0]
    tc = _divisor_tile(t_len, 512, V7X_SUBLANES)
    hpc = _divisor_tile(heads, max(1, 1024 // hd), 1)
    wd = hpc * hd
    ncell = heads // hpc
    vmem = 2 * 3 * tc * wd * 4 + (tc + 8) * wd * 4 + 2 * tc * wd * 4 + 2 * hpc * hd * 2 * hd * 2 + 8 * tc * hd * 4
    body = functools.partial(_rg_body, conv_width=conv_width)
    hsel = lambda b, h, t: (0, 0, h)
    return pl.pallas_call(
        body,
        grid=(bsz, ncell, t_len // tc),
        in_specs=[
            pl.BlockSpec((None, tc, wd), lambda b, h, t: (b, t, h)),
            pl.BlockSpec((None, tc, wd), lambda b, h, t: (b, t, ncell + h)),
            pl.BlockSpec((conv_width, wd), lambda b, h, t: (0, h)),
            pl.BlockSpec((1, wd), lambda b, h, t: (0, h)),
            pl.BlockSpec((hpc, hd, 2 * hd), lambda b, h, t: (h, 0, 0)),
            pl.BlockSpec((hpc, 1, 2 * hd), lambda b, h, t: (h, 0, 0)),
            pl.BlockSpec((1, wd), lambda b, h, t: (0, h)),
            pl.BlockSpec((None, 8, wd), hsel),
            pl.BlockSpec((None, 8, wd), hsel),
        ],
        out_specs=[
            pl.BlockSpec((None, tc, wd), lambda b, h, t: (b, t, h)),
            pl.BlockSpec((None, 8, wd), lambda b, h, t: (b, 0, h)),
            pl.BlockSpec((None, 8, wd), lambda b, h, t: (b, 0, h)),
        ],
        out_shape=[
            jax.ShapeDtypeStruct((bsz, t_len, d_rg), F32),
            jax.ShapeDtypeStruct((bsz, 8, d_rg), F32),
            jax.ShapeDtypeStruct((bsz, 8, d_rg), F32),
        ],
        scratch_shapes=[
            pltpu.VMEM((tc + 8, wd), F32),
            pltpu.VMEM((tc, wd), F32),
            pltpu.VMEM((tc, wd), F32),
            pltpu.VMEM((8, wd), F32),
        ],
        compiler_params=_params(("parallel", "parallel", "arbitrary"), vmem),
        name="rg_lru",
    )(proj, proj, conv_w, conv_b.reshape(1, d_rg), wax, bax, softplus_neg_lam.reshape(1, d_rg),
      h0, tail0)


def _inproj_rg_body(x_ref, g_ref, w_ref, cw_ref, cb_ref, wax_ref, bax_ref, sp_ref, h0_ref, tail0_ref,
                    y_ref, ps5_ref, ht_ref, tailt_ref,
                    hn_ref, ug_ref, ext_ref, a_ref, b_ref, hc_ref, tail_ref, *, heads, conv_width):
    t = pl.program_id(1)
    s = pl.program_id(2)
    tc, hd = y_ref.shape
    ng = tc // V7X_SUBLANES
    rc = _row_chunk(tc)
    hh = jnp.clip(s - 1, 0, heads - 1)
    valid = jnp.logical_and(s >= 1, s <= heads)

    @pl.when(s == 0)
    def _():
        def body(c, _):
            r0 = pl.multiple_of(c * rc, rc)
            hn_ref[pl.ds(r0, rc), :] = _rms_rows(x_ref[pl.ds(r0, rc), :], g_ref[...]).astype(BF16)
            return 0

        lax.fori_loop(0, tc // rc, body, 0, unroll=2)
        ug_ref[1] = jnp.zeros(ug_ref.shape[1:], F32)

        @pl.when(t == 0)
        def _():
            hc_ref[...] = jnp.zeros_like(hc_ref)
            tail_ref[...] = jnp.zeros_like(tail_ref)

    prev = ug_ref.at[lax.rem(s + 1, 2)]
    use_init = jnp.logical_and(t == 0, s <= heads)
    hc0 = jnp.where(use_init, h0_ref[...], hc_ref[hh])
    ext_ref[0:8, :] = jnp.where(use_init, tail0_ref[...], tail_ref[hh])
    ext_ref[8:8 + tc, :] = prev[:, :hd]
    xc = cb_ref[...] + cw_ref[conv_width - 1:conv_width, :] * prev[:, :hd]
    for k in range(conv_width - 1):
        back = conv_width - 1 - k
        xc = xc + cw_ref[k:k + 1, :] * ext_ref[pl.ds(8 - back, tc), :]
    pre = jnp.dot(xc.astype(BF16), wax_ref[...], preferred_element_type=F32) + bax_ref[...]
    r = _sigmoid(pre[:, :hd])
    i = _sigmoid(pre[:, hd:])
    a = jnp.exp((-RG_C) * r * sp_ref[...])
    b = jnp.sqrt(1.0 - a * a) * i * xc
    a3, b3 = _group_scan_real(a.reshape(ng, 8, hd), b.reshape(ng, 8, hd))
    a_ref[...] = a3.reshape(tc, hd)
    b_ref[...] = b3.reshape(tc, hd)
    keep_old = s > heads

    def body(g, hc):
        r0 = pl.multiple_of(g * 8, 8)
        h = a_ref[pl.ds(r0, 8), :] * hc + b_ref[pl.ds(r0, 8), :]
        y_new = h * _gelu_tanh(prev[pl.ds(r0, 8), hd:])
        y_ref[pl.ds(r0, 8), :] = jnp.where(keep_old, y_ref[pl.ds(r0, 8), :], y_new)
        return jnp.broadcast_to(h[7:8, :], (8, hd))

    hc = lax.fori_loop(0, ng, body, hc0, unroll=True)
    hc = jnp.where(valid, hc, hc0)
    tail = jnp.where(valid, ext_ref[tc:tc + 8, :], ext_ref[0:8, :])
    hc_ref[hh] = hc
    tail_ref[hh] = tail
    ht_ref[...] = hc
    tailt_ref[...] = tail

    res = jnp.dot(hn_ref[...], w_ref[...], preferred_element_type=F32)
    ug_ref[lax.rem(s, 2)] = res
    ps5_ref[...] = res


def _inproj_rg(x, norm_w, w_tiles, conv_w, conv_b, wax, bax, softplus_neg_lam, h0, tail0, *, bsz, d_s5):
    m, d = x.shape
    t_len = m // bsz
    heads, hd, _ = wax.shape
    conv_width = conv_w.shape[1]
    tw = 2 * hd
    assert d_s5 % tw == 0, (d_s5, tw)
    n_steps = heads + d_s5 // tw
    assert n_steps >= heads + 1
    tc = _divisor_tile(t_len, 512, 16)
    nt = t_len // tc
    vmem = (2 * tc * d * 4 + tc * d * 2 + 2 * d * tw * 2 + 2 * tc * tw * 4 + 2 * tc * hd * 4 + 2 * tc * tw * 4
            + 3 * (tc + 8) * hd * 4 + 8 * tc * tw * 4)
    body = functools.partial(_inproj_rg_body, heads=heads, conv_width=conv_width)
    head_of = lambda s: jnp.clip(s - 1, 0, heads - 1)
    per_head = lambda rows, cols: pl.BlockSpec((None, rows, cols), lambda b, t, s: (head_of(s), 0, 0))
    state_out = pl.BlockSpec((None, None, None, 8, hd), lambda b, t, s: (b, t, head_of(s), 0, 0))
    return pl.pallas_call(
        body,
        grid=(bsz, nt, n_steps),
        in_specs=[
            pl.BlockSpec((tc, d), lambda b, t, s: (b * nt + t, 0)),
            pl.BlockSpec((1, d), lambda b, t, s: (0, 0)),
            pl.BlockSpec((d, tw), lambda b, t, s: (0, s)),
            per_head(conv_width, hd), per_head(1, hd), per_head(hd, tw), per_head(1, tw), per_head(1, hd),
            per_head(8, hd), per_head(8, hd),
        ],
        out_specs=[
            pl.BlockSpec((None, tc, hd), lambda b, t, s: (b, t, head_of(s))),
            pl.BlockSpec((None, tc, tw), lambda b, t, s: (b, t, jnp.maximum(s - heads, 0))),
            state_out, state_out,
        ],
        out_shape=[
            jax.ShapeDtypeStruct((bsz, t_len, heads * hd), F32),
            jax.ShapeDtypeStruct((bsz, t_len, d_s5), F32),
            jax.ShapeDtypeStruct((bsz, nt, heads, 8, hd), F32),
            jax.ShapeDtypeStruct((bsz, nt, heads, 8, hd), F32),
        ],
        scratch_shapes=[
            pltpu.VMEM((tc, d), BF16),
            pltpu.VMEM((2, tc, tw), F32),
            pltpu.VMEM((tc + 8, hd), F32),
            pltpu.VMEM((tc, hd), F32),
            pltpu.VMEM((tc, hd), F32),
            pltpu.VMEM((heads, 8, hd), F32),
            pltpu.VMEM((heads, 8, hd), F32),
        ],
        compiler_params=_params(("arbitrary", "arbitrary", "arbitrary"), vmem),
        name="inproj_rg",
    )(x, norm_w.reshape(1, d), w_tiles, conv_w, conv_b, wax, bax, softplus_neg_lam, h0, tail0)


def _s5_body(u_ref, km_ref, wsm_ref, vm_ref, e1_ref, e2_ref, mm_ref, mw_ref, mv_ref, lam_ref, p8_ref, d_ref,
             x0_ref, z_ref, xt_ref, mi_ref, ws_ref, v_ref, s_ref, xc_ref):
    t = pl.program_id(2)
    tc, slab = u_ref.shape
    nc = tc // S5_CHUNK
    ns = lam_ref.shape[1] // 2
    ng = nc // V7X_SUBLANES

    @pl.when(jnp.logical_and(pl.program_id(1) == 0, t == 0))
    def _():
        def expand(small_ref, e_ref, mask_ref, dst_ref):
            width = dst_ref.shape[1]
            piece = min(width, 2 * V7X_LANES)
            small = small_ref[...].astype(BF16)
            for q0 in range(0, width, piece):
                full = jnp.dot(small, e_ref[:, q0:q0 + piece], preferred_element_type=F32)
                dst_ref[:, q0:q0 + piece] = (full * mask_ref[:, q0:q0 + piece].astype(F32)).astype(BF16)

        expand(km_ref, e1_ref, mm_ref, mi_ref)
        expand(wsm_ref, e2_ref, mw_ref, ws_ref)
        expand(vm_ref, e1_ref, mv_ref, v_ref)

    @pl.when(t == 0)
    def _():
        xc_ref[...] = x0_ref[...]

    us = [u_ref[pl.ds(i, nc, stride=S5_CHUNK), :] for i in range(S5_CHUNK)]
    ucat = jnp.concatenate([ui.astype(BF16) for ui in us], axis=-1)
    send = jnp.dot(ucat, ws_ref[...], preferred_element_type=F32)
    ycat = jnp.dot(ucat, mi_ref[...], preferred_element_type=F32)

    xr, xi = _group_scan_const_complex(send[:, :ns].reshape(ng, 8, ns), send[:, ns:].reshape(ng, 8, ns),
                                       lam_ref, ns)
    s_ref[7:8, :] = xc_ref[0:1, :]
    s_ref[8:8 + nc, :ns] = xr.reshape(nc, ns)
    s_ref[8:8 + nc, ns:] = xi.reshape(nc, ns)

    p8r = p8_ref[:, :ns]
    p8i = p8_ref[:, ns:]

    def body(g, carry):
        cr, ci = carry
        r0 = pl.multiple_of(8 + g * 8, 8)
        nr = s_ref[pl.ds(r0, 8), :ns] + (p8r * cr - p8i * ci)
        ni = s_ref[pl.ds(r0, 8), ns:] + (p8r * ci + p8i * cr)
        s_ref[pl.ds(r0, 8), :ns] = nr
        s_ref[pl.ds(r0, 8), ns:] = ni
        return (jnp.broadcast_to(nr[7:8, :], (8, ns)), jnp.broadcast_to(ni[7:8, :], (8, ns)))

    cr, ci = lax.fori_loop(0, ng, body, (xc_ref[:, :ns], xc_ref[:, ns:]), unroll=True)
    xc_ref[:, :ns] = cr
    xc_ref[:, ns:] = ci

    x_in = s_ref[pl.ds(7, nc), :]
    ycat = ycat + jnp.dot(x_in.astype(BF16), v_ref[...], preferred_element_type=F32)
    for i in range(S5_CHUNK):
        y = ycat[:, i * slab:(i + 1) * slab] + d_ref[...] * us[i]
        z_ref[pl.ds(i, nc, stride=S5_CHUNK), :] = _gelu_tanh(y)

    @pl.when(t == pl.num_programs(2) - 1)
    def _():
        xt_ref[...] = xc_ref[...]


def _s5_scan(proj, tables, consts, d_skip, x0, *, col0, d_s5):
    bsz, t_len, _ = proj.shape
    k_m, w_m, v_m, lam_pow, p8 = tables
    e1, e2, mask_m, mask_w, mask_v = consts
    nslab, kdim, _ = k_m.shape
    slab = kdim // S5_CHUNK
    ns2 = lam_pow.shape[-1]
    tc = _divisor_tile(t_len, 4096, S5_CHUNK * V7X_SUBLANES)
    assert tc % (S5_CHUNK * V7X_SUBLANES) == 0, (t_len, tc)
    nc = tc // S5_CHUNK
    cb0 = col0 // slab
    big = 2 * kdim * kdim + kdim * ns2
    vmem = (4 * tc * slab * 4 + 2 * big * 2 + big * 2 + 2 * (2 * kdim + ns2) * slab * 4 + (nc + 8) * ns2 * 4
            + nc * (2 * kdim * 4 + kdim * 2 + 3 * ns2 * 4))
    slab_blk = lambda a: pl.BlockSpec((None,) + a.shape[1:], lambda s, b, t: (s, 0, 0))
    const_blk = lambda a: pl.BlockSpec(a.shape, lambda s, b, t: (0, 0))
    return pl.pallas_call(
        _s5_body,
        grid=(nslab, bsz, t_len // tc),
        in_specs=[
            pl.BlockSpec((None, tc, slab), lambda s, b, t: (b, t, cb0 + s)),
            slab_blk(k_m), slab_blk(w_m), slab_blk(v_m),
            const_blk(e1), const_blk(e2), const_blk(mask_m), const_blk(mask_w), const_blk(mask_v),
            pl.BlockSpec((None, 8, ns2), lambda s, b, t: (s, 0, 0)),
            pl.BlockSpec((None, 8, ns2), lambda s, b, t: (s, 0, 0)),
            pl.BlockSpec((1, slab), lambda s, b, t: (0, s)),
            pl.BlockSpec((None, None, 8, ns2), lambda s, b, t: (0, s, 0, 0)),
        ],
        out_specs=[
            pl.BlockSpec((None, tc, slab), lambda s, b, t: (b, t, s)),
            pl.BlockSpec((None, None, 8, ns2), lambda s, b, t: (b, s, 0, 0)),
        ],
        out_shape=[
            jax.ShapeDtypeStruct((bsz, t_len, d_s5), F32),
            jax.ShapeDtypeStruct((bsz, nslab, 8, ns2), F32),
        ],
        scratch_shapes=[
            pltpu.VMEM((kdim, kdim), BF16),
            pltpu.VMEM((kdim, ns2), BF16),
            pltpu.VMEM((ns2, kdim), BF16),
            pltpu.VMEM((nc + 8, ns2), F32),
            pltpu.VMEM((8, ns2), F32),
        ],
        compiler_params=_params(("parallel", "arbitrary", "arbitrary"), vmem),
        name="s5_scan",
    )(proj, k_m, w_m, v_m, e1, e2, mask_m, mask_w, mask_v, lam_pow, p8, d_skip.reshape(1, d_s5), x0)


def _glu_body(z_ref, w_ref, b_ref, o_ref, zb_ref):
    tm = z_ref.shape[0]
    tn = o_ref.shape[1]
    rc = _row_chunk(tm)

    @pl.when(pl.program_id(1) == 0)
    def _():
        def body(c, _):
            r0 = pl.multiple_of(c * rc, rc)
            zb_ref[pl.ds(r0, rc), :] = z_ref[pl.ds(r0, rc), :].astype(BF16)
            return 0

        lax.fori_loop(0, tm // rc, body, 0, unroll=2)

    pre = jnp.dot(zb_ref[...], w_ref[...], preferred_element_type=F32) + b_ref[...]
    cols = pl.ds(pl.multiple_of(pl.program_id(1) * tn, tn), tn)
    o_ref[...] = z_ref[:, cols] * _sigmoid(pre)


def _glu(z, w, b):
    m, d = z.shape
    tm = _divisor_tile(m, 1024, 16)
    tn = _divisor_tile(d, 1024, V7X_LANES)
    vmem = 2 * tm * d * 4 + tm * d * 2 + 2 * d * tn * 2 + 4 * tm * tn * 4
    return pl.pallas_call(
        _glu_body,
        grid=(m // tm, d // tn),
        in_specs=[
            pl.BlockSpec((tm, d), lambda i, j: (i, 0)),
            pl.BlockSpec((d, tn), lambda i, j: (0, j)),
            pl.BlockSpec((1, tn), lambda i, j: (0, j)),
        ],
        out_specs=pl.BlockSpec((tm, tn), lambda i, j: (i, j)),
        out_shape=jax.ShapeDtypeStruct((m, d), F32),
        scratch_shapes=[pltpu.VMEM((tm, d), BF16)],
        compiler_params=_params(("parallel", "arbitrary"), vmem),
        name="s5_glu",
    )(z, w, b.reshape(1, d))


def _outproj_body(yr_hbm, ys_hbm, gr_ref, gs_ref, w_ref, h_ref, o_ref, yn_ref, rbuf, sbuf, rsem, ssem):
    tm = yn_ref.shape[0]
    d_rg = rbuf.shape[-1]
    rc = _row_chunk(tm)

    def consume():
        def body(c, _):
            rows = pl.ds(pl.multiple_of(c * rc, rc), rc)
            yn_ref[rows, :d_rg] = _rms_rows(rbuf[rows, :], gr_ref[...]).astype(BF16)
            yn_ref[rows, d_rg:] = _rms_rows(sbuf[rows, :], gs_ref[...]).astype(BF16)
            return 0

        lax.fori_loop(0, tm // rc, body, 0, unroll=2)

    _staged_rows([yr_hbm, ys_hbm], [rbuf, sbuf], [rsem, ssem], consume)
    o_ref[...] = h_ref[...] + jnp.dot(yn_ref[...], w_ref[...], preferred_element_type=F32)


def _outproj(y_rg, y_s5, g_rg, g_s5, w, h):
    m, d_rg = y_rg.shape
    d_s5 = y_s5.shape[1]
    d_mix, d = w.shape
    tm = _divisor_tile(m, 1024, 16)
    tn = _divisor_tile(d, min(512, d // 2), V7X_LANES)
    assert d // tn >= 2, (d, tn)
    vmem = tm * d_mix * 4 + tm * d_mix * 2 + 2 * d_mix * tn * 2 + 5 * tm * tn * 4
    return pl.pallas_call(
        _outproj_body,
        grid=(m // tm, d // tn),
        in_specs=[
            pl.BlockSpec(memory_space=pl.ANY),
            pl.BlockSpec(memory_space=pl.ANY),
            pl.BlockSpec((1, d_rg), lambda i, j: (0, 0)),
            pl.BlockSpec((1, d_s5), lambda i, j: (0, 0)),
            pl.BlockSpec((d_mix, tn), lambda i, j: (0, j)),
            pl.BlockSpec((tm, tn), lambda i, j: (i, j)),
        ],
        out_specs=pl.BlockSpec((tm, tn), lambda i, j: (i, j)),
        out_shape=jax.ShapeDtypeStruct((m, d), F32),
        scratch_shapes=[
            pltpu.VMEM((tm, d_mix), BF16),
            pltpu.VMEM((tm, d_rg), F32),
            pltpu.VMEM((tm, d_s5), F32),
            pltpu.SemaphoreType.DMA((1,)),
            pltpu.SemaphoreType.DMA((1,)),
        ],
        compiler_params=_params(("arbitrary", "arbitrary"), vmem),
        name="out_proj",
    )(y_rg, y_s5, g_rg.reshape(1, d_rg), g_s5.reshape(1, d_s5), w, h)


def _s5_tables(lam_re, lam_im, log_dt, b_re, b_im, c_re, c_im, slab):
    g, n = lam_re.shape
    c = b_re.shape[-1]
    gs = slab // c
    nslab = g // gs
    ell = S5_CHUNK
    dt = jnp.exp(log_dt.astype(F32))[:, None]
    lam = lax.complex(lam_re.astype(F32), lam_im.astype(F32))
    lam_dt = lam * dt
    lam_bar = jnp.exp(lam_dt)
    b_bar = ((lam_bar - 1.0) / lam)[..., None] * lax.complex(b_re.astype(F32), b_im.astype(F32))
    cc = lax.complex(c_re.astype(F32), c_im.astype(F32))

    def lam_pow(ks):
        return jnp.exp(lam_dt[None] * jnp.asarray(ks, F32)[:, None, None])

    def state_lanes(p):
        p = p.reshape(p.shape[0], nslab, gs * n)
        return jnp.concatenate([jnp.real(p), jnp.imag(p)], axis=-1).transpose(1, 0, 2)

    lam_tab = state_lanes(lam_pow([ell, 2 * ell, 4 * ell] + [ell] * 5))
    p8 = state_lanes(lam_pow([ell * (r + 1) for r in range(8)]))

    pw = lam_pow(list(range(ell + 1)))
    exact = lax.Precision.HIGHEST

    def slab_major(a):
        return a.reshape(ell, nslab, gs * c, a.shape[-1]).transpose(1, 0, 2, 3).reshape(nslab, ell * gs * c, a.shape[-1])

    kk = jnp.real(jnp.einsum('gdn,kgn,gnc->gckd', cc, pw[:ell], b_bar)).reshape(g * c, ell * c)
    place = np.zeros((ell, ell * c, ell * c), np.float32)
    for i in range(ell):
        for k in range(ell - i):
            place[i, k * c + np.arange(c), (i + k) * c + np.arange(c)] = 1.0
    k_m = slab_major(jnp.einsum('rk,ikq->irq', kk, jnp.asarray(place), precision=exact))
    p_st = pw[ell - 1 - np.arange(ell)][:, :, None, :]
    b_t = b_bar.transpose(0, 2, 1)[None]
    pr, pi, br, bi = jnp.real(p_st), jnp.imag(p_st), jnp.real(b_t), jnp.imag(b_t)
    w_m = slab_major(jnp.concatenate([pr * br - pi * bi, pr * bi + pi * br], axis=-1).reshape(ell, g * c, 2 * n))
    rep_i = np.kron(np.eye(ell, dtype=np.float32), np.ones((1, c), np.float32))
    rep_c = np.kron(np.ones((1, ell), np.float32), np.eye(c, dtype=np.float32))
    p_out = pw[1:ell + 1].transpose(1, 2, 0).reshape(g * n, ell)
    c_t = cc.transpose(0, 2, 1).reshape(g * n, c)
    ar, ai = (jnp.dot(part(p_out), rep_i, precision=exact) for part in (jnp.real, jnp.imag))
    cr, ci = (jnp.dot(part(c_t), rep_c, precision=exact) for part in (jnp.real, jnp.imag))
    v_m = jnp.concatenate([(cr * ar - ci * ai).reshape(nslab, gs * n, ell * c),
                           -(cr * ai + ci * ar).reshape(nslab, gs * n, ell * c)], axis=1)
    return k_m, w_m, v_m, lam_tab, p8


def _s5_spread_consts(gs, c, n):
    ell = S5_CHUNK
    a1, q1 = np.arange(ell * c)[:, None], np.arange(ell * gs * c)[None, :]
    e1 = (a1 // c == q1 // (gs * c)) & (a1 % c == q1 % c)
    a2, q2 = np.arange(2 * n)[:, None], np.arange(2 * gs * n)[None, :]
    e2 = (a2 // n == q2 // (gs * n)) & (a2 % n == q2 % n)
    g_in = (np.arange(ell * gs * c) // c) % gs
    g_st = (np.arange(2 * gs * n) % (gs * n)) // n
    mask_m = g_in[:, None] == g_in[None, :]
    mask_w = g_in[:, None] == g_st[None, :]
    mask_v = g_st[:, None] == g_in[None, :]
    return tuple(jnp.asarray(m, dtype=BF16) for m in (e1, e2, mask_m, mask_w, mask_v))


def kernel(x, meta_tokens, ffn1_norm, ffn1_w_gate, ffn1_w_up, ffn1_w_down, mix_norm, w_in,
           rg_conv_w, rg_conv_b, rg_w_a, rg_b_a, rg_w_x, rg_b_x, rg_lambda,
           s5_lambda_re, s5_lambda_im, s5_log_dt, s5_b_re, s5_b_im, s5_c_re, s5_c_im, s5_d,
           s5_glu_w, s5_glu_b, rg_out_norm, s5_out_norm, w_out,
           ffn2_norm, ffn2_w_gate, ffn2_w_up, ffn2_w_down, final_norm):
    bsz, t_len, d = x.shape
    depth = ffn1_norm.shape[0]
    d_rg = rg_lambda.shape[-1]
    d_s5 = s5_d.shape[-1]
    heads, hd = rg_w_a.shape[1], rg_w_a.shape[2]
    slab = V7X_LANES
    s5_align = S5_CHUNK * V7X_SUBLANES

    h = x.reshape(bsz * t_len, d)
    hm = meta_tokens.astype(x.dtype)
    n_meta = hm.shape[0]
    for l in range(depth):
        last = l == depth - 1
        ffn1_w = (ffn1_w_gate[l].astype(BF16), ffn1_w_up[l].astype(BF16), ffn1_w_down[l].astype(BF16))
        ffn2_w = (ffn2_w_gate[l].astype(BF16), ffn2_w_up[l].astype(BF16), ffn2_w_down[l].astype(BF16))
        w_u = w_in[l][:, :d_rg].reshape(d, heads, hd)
        w_g = w_in[l][:, d_rg:2 * d_rg].reshape(d, heads, hd)
        w_in_b = jnp.concatenate([jnp.concatenate([w_u, w_g], axis=-1).reshape(d, 2 * d_rg), w_in[l][:, 2 * d_rg:]],
                                 axis=-1).astype(BF16)
        conv_w_h = rg_conv_w[l].reshape(-1, heads, hd).transpose(1, 0, 2)
        conv_b_h = rg_conv_b[l].reshape(heads, 1, hd)
        w_out_b = w_out[l].astype(BF16)
        glu_w_b = s5_glu_w[l].astype(BF16)
        wax = jnp.concatenate([rg_w_a[l], rg_w_x[l]], axis=-1).astype(BF16)
        bax = jnp.concatenate([rg_b_a[l], rg_b_x[l]], axis=-1).reshape(heads, 1, 2 * hd)
        sp = jax.nn.softplus(-rg_lambda[l].astype(F32)).reshape(heads, 1, hd)
        s5_mats = _s5_tables(s5_lambda_re[l], s5_lambda_im[l], s5_log_dt[l], s5_b_re[l], s5_b_im[l],
                             s5_c_re[l], s5_c_im[l], slab)
        nslab, _, ns2 = s5_mats[3].shape
        s5_consts = _s5_spread_consts(slab // s5_b_re.shape[-1], s5_b_re.shape[-1], s5_lambda_re.shape[-1])
        fnorm = final_norm if last else ffn2_norm[l]

        def mixers(hrows, nb, h0, tail0, x0, front_pad):
            h1 = _ffn(hrows, ffn1_norm[l], *ffn1_w, ffn1_norm[l], final_norm=False)
            y_rg, proj_s5, h_t, tail_t = _inproj_rg(h1, mix_norm[l], w_in_b, conv_w_h, conv_b_h, wax, bax, sp,
                                                    h0, tail0, bsz=nb, d_s5=d_s5)
            if front_pad:
                proj_s5 = jnp.pad(proj_s5, ((0, 0), (front_pad, 0), (0, 0)))
            z, x_t = _s5_scan(proj_s5, s5_mats, s5_consts, s5_d[l], x0, col0=0, d_s5=d_s5)
            return h1, y_rg, z[:, front_pad:], h_t[0, -1], tail_t[0, -1], x_t

        def tail_layers(h1, y_rg, z):
            m = h1.shape[0]
            y_s5 = _glu(z.reshape(m, d_s5), glu_w_b, s5_glu_b[l])
            h2 = _outproj(y_rg.reshape(m, d_rg), y_s5, rg_out_norm[l], s5_out_norm[l], w_out_b, h1)
            return _ffn(h2, ffn2_norm[l], *ffn2_w, fnorm, final_norm=last)

        assert t_len % s5_align == 0, t_len
        zeros_rg = jnp.zeros((heads, 8, hd), F32)
        zeros_s5 = jnp.zeros((1, nslab, 8, ns2), F32)
        meta_pad = (-n_meta) % s5_align
        h1m, y_rgm, zm, h_t, tail_t, x_t = mixers(hm, 1, zeros_rg, zeros_rg, zeros_s5, meta_pad)
        h1, y_rg, z, _, _, _ = mixers(h, bsz, h_t, tail_t, x_t, 0)
        h = tail_layers(h1, y_rg, z)
        if not last:
            hm = tail_layers(h1m, y_rgm, zm)
    return h.reshape(bsz, t_len, d)
```

```python
import functools
import math

import jax
import jax.numpy as jnp
import numpy as np
from jax import lax
from jax.experimental import pallas as pl
from jax.experimental.pallas import tpu as pltpu

EPS = 1e-6
RG_C = 8.0
F32 = jnp.float32
BF16 = jnp.bfloat16

V7X_LANES = 128
V7X_SUBLANES = 8
V7X_VMEM_BYTES = 64 * 1024 * 1024

S5_CHUNK = 8


def _divisor_tile(n, pref, align):
    if n <= pref:
        return n
    t = (pref // align) * align
    while t >= align:
        if n % t == 0:
            return t
        t -= align
    return n


def _params(sem, vmem_bytes):
    limit = int(min(V7X_VMEM_BYTES - (2 << 20), max(vmem_bytes + (6 << 20), 32 << 20)))
    return pltpu.CompilerParams(dimension_semantics=sem, vmem_limit_bytes=limit)


def _rms_rows(x, g):
    ms = jnp.mean(x * x, axis=-1, keepdims=True)
    return x * lax.rsqrt(ms + EPS) * g


def _sigmoid(x):
    return 1.0 / (1.0 + jnp.exp(-x))


def _gelu_tanh(x):
    c = math.sqrt(2.0 / math.pi)
    return 0.5 * x * (1.0 + jnp.tanh(c * (x + 0.044715 * (x * x * x))))


def _row_chunk(tm):
    return _divisor_tile(tm, 32, V7X_SUBLANES)


def _ffn_body(x_hbm, g_ref, wg_ref, wu_ref, wd_ref, g2_ref, o_ref, hn_ref, sem, *, n_split, final_norm, n_chunks):
    i = pl.program_id(0)
    j = pl.program_id(1)
    tm, d = o_ref.shape
    ck = tm // n_chunks
    rc = _row_chunk(ck)

    def x_copy(c):
        rows = pl.ds(pl.multiple_of(i * tm + c * ck, ck), ck)
        return pltpu.make_async_copy(x_hbm.at[rows, :], o_ref.at[pl.ds(c * ck, ck), :], sem.at[c])

    @pl.when(j == 0)
    def _():
        for c in range(n_chunks):
            x_copy(c).start()
        for c in range(n_chunks):
            x_copy(c).wait()

            def body(r, _):
                r0 = pl.multiple_of(c * ck + r * rc, rc)
                hn_ref[pl.ds(r0, rc), :] = _rms_rows(o_ref[pl.ds(r0, rc), :], g_ref[...]).astype(BF16)
                return 0

            lax.fori_loop(0, ck // rc, body, 0, unroll=2)

    g = jnp.dot(hn_ref[...], wg_ref[...], preferred_element_type=F32)
    u = jnp.dot(hn_ref[...], wu_ref[...], preferred_element_type=F32)
    a = (0.5 * (g * _sigmoid(g) * u)).astype(BF16)
    dn = d // n_split
    for s in range(n_split):
        o_ref[:, s * dn:(s + 1) * dn] += jnp.dot(a, wd_ref[:, s * dn:(s + 1) * dn],
                                                 preferred_element_type=F32)

    if final_norm:
        @pl.when(j == pl.num_programs(1) - 1)
        def _():
            def body(c, _):
                r0 = pl.multiple_of(c * rc, rc)
                o_ref[pl.ds(r0, rc), :] = _rms_rows(o_ref[pl.ds(r0, rc), :], g2_ref[...])
                return 0

            lax.fori_loop(0, tm // rc, body, 0, unroll=2)


def _ffn(x, norm_w, wg, wu, wd, norm2_w, *, final_norm):
    m, d = x.shape
    f = wg.shape[1]
    tf = _divisor_tile(f, 256, V7X_LANES)
    tm = _divisor_tile(m, 1024, 16)
    n_chunks = max(1, tm // 128)
    n_split = max(1, d // 1024)
    vmem = 2 * tm * d * 4 + tm * d * 2 + 6 * d * tf * 2 + tm * tf * 4 * 4
    body = functools.partial(_ffn_body, n_split=n_split, final_norm=final_norm, n_chunks=n_chunks)
    return pl.pallas_call(
        body,
        grid=(m // tm, f // tf),
        in_specs=[
            pl.BlockSpec(memory_space=pl.ANY),
            pl.BlockSpec((1, d), lambda i, j: (0, 0)),
            pl.BlockSpec((d, tf), lambda i, j: (0, j)),
            pl.BlockSpec((d, tf), lambda i, j: (0, j)),
            pl.BlockSpec((tf, d), lambda i, j: (j, 0)),
            pl.BlockSpec((1, d), lambda i, j: (0, 0)),
        ],
        out_specs=pl.BlockSpec((tm, d), lambda i, j: (i, 0)),
        out_shape=jax.ShapeDtypeStruct((m, d), F32),
        scratch_shapes=[pltpu.VMEM((tm, d), BF16), pltpu.SemaphoreType.DMA((n_chunks,))],
        compiler_params=_params(("parallel", "arbitrary"), vmem),
        name="ffn_swiglu",
    )(x, norm_w.reshape(1, d), wg, wu, wd, norm2_w.reshape(1, d))


def _staged_rows(srcs, bufs, sems, consume):
    i = pl.program_id(0)
    j = pl.program_id(1)
    tm = bufs[0].shape[0]

    def copies(tile):
        rows = pl.ds(pl.multiple_of(tile * tm, tm), tm)
        return [pltpu.make_async_copy(s.at[rows, :], b, m.at[0]) for s, b, m in zip(srcs, bufs, sems)]

    @pl.when(j == 0)
    def _():
        @pl.when(i == 0)
        def _():
            for cp in copies(0):
                cp.start()

        for cp in copies(i):
            cp.wait()
        consume()

    @pl.when(jnp.logical_and(j == 1, i + 1 < pl.num_programs(0)))
    def _():
        for cp in copies(i + 1):
            cp.start()


def _inproj_body(x_hbm, g_ref, w_ref, o_ref, hn_ref, xbuf, sem):
    tm = hn_ref.shape[0]
    rc = _row_chunk(tm)

    def consume():
        def body(c, _):
            r0 = pl.multiple_of(c * rc, rc)
            hn_ref[pl.ds(r0, rc), :] = _rms_rows(xbuf[pl.ds(r0, rc), :], g_ref[...]).astype(BF16)
            return 0

        lax.fori_loop(0, tm // rc, body, 0, unroll=2)

    _staged_rows([x_hbm], [xbuf], [sem], consume)
    o_ref[...] = jnp.dot(hn_ref[...], w_ref[...], preferred_element_type=F32)


def _inproj(x, norm_w, w):
    m, d = x.shape
    n = w.shape[1]
    tm = _divisor_tile(m, 1024, 16)
    tn = _divisor_tile(n, min(1024, n // 2), V7X_LANES)
    assert n // tn >= 2, (n, tn)
    vmem = tm * d * 4 + tm * d * 2 + 2 * d * tn * 2 + 3 * tm * tn * 4
    return pl.pallas_call(
        _inproj_body,
        grid=(m // tm, n // tn),
        in_specs=[
            pl.BlockSpec(memory_space=pl.ANY),
            pl.BlockSpec((1, d), lambda i, j: (0, 0)),
            pl.BlockSpec((d, tn), lambda i, j: (0, j)),
        ],
        out_specs=pl.BlockSpec((tm, tn), lambda i, j: (i, j)),
        out_shape=jax.ShapeDtypeStruct((m, n), F32),
        scratch_shapes=[pltpu.VMEM((tm, d), BF16), pltpu.VMEM((tm, d), F32), pltpu.SemaphoreType.DMA((1,))],
        compiler_params=_params(("arbitrary", "arbitrary"), vmem),
        name="in_proj",
    )(x, norm_w.reshape(1, d), w)


def _group_scan_real(a, b):
    row = lax.broadcasted_iota(jnp.int32, a.shape, 1)
    for dist in (1, 2, 4):
        keep = row >= dist
        a_prev = jnp.where(keep, pltpu.roll(a, dist, axis=1), 1.0)
        b_prev = jnp.where(keep, pltpu.roll(b, dist, axis=1), 0.0)
        b = a * b_prev + b
        a = a * a_prev
    return a, b


def _group_scan_const_complex(xr, xi, lam_ref, ns):
    row = lax.broadcasted_iota(jnp.int32, xr.shape, 1)
    for k, dist in enumerate((1, 2, 4)):
        lr = lam_ref[k:k + 1, :ns]
        li = lam_ref[k:k + 1, ns:]
        keep = row >= dist
        pr = jnp.where(keep, pltpu.roll(xr, dist, axis=1), 0.0)
        pi = jnp.where(keep, pltpu.roll(xi, dist, axis=1), 0.0)
        xr, xi = xr + (lr * pr - li * pi), xi + (lr * pi + li * pr)
    return xr, xi


def _rg_body(u_ref, gate_ref, cw_ref, cb_ref, wax_ref, bax_ref, sp_ref, h0_ref, tail0_ref,
             y_ref, ht_ref, tailt_ref, ext_ref, a_ref, b_ref, hc_ref, *, conv_width):
    t = pl.program_id(2)
    tc, width = u_ref.shape
    hpc, hd, _ = wax_ref.shape
    ng = tc // V7X_SUBLANES

    @pl.when(t == 0)
    def _():
        ext_ref[0:8, :] = tail0_ref[...]
        hc_ref[...] = h0_ref[...]

    ext_ref[8:8 + tc, :] = u_ref[...]
    for hh in range(hpc):
        cols = slice(hh * hd, (hh + 1) * hd)
        xc = cb_ref[:, cols] + cw_ref[conv_width - 1:conv_width, cols] * u_ref[:, cols]
        for k in range(conv_width - 1):
            back = conv_width - 1 - k
            xc = xc + cw_ref[k:k + 1, cols] * ext_ref[pl.ds(8 - back, tc), cols]
        pre = jnp.dot(xc.astype(BF16), wax_ref[hh], preferred_element_type=F32) + bax_ref[hh]
        r = _sigmoid(pre[:, :hd])
        i = _sigmoid(pre[:, hd:])
        a = jnp.exp((-RG_C) * r * sp_ref[:, cols])
        b = jnp.sqrt(1.0 - a * a) * i * xc
        a3, b3 = _group_scan_real(a.reshape(ng, 8, hd), b.reshape(ng, 8, hd))
        a_ref[:, cols] = a3.reshape(tc, hd)
        b_ref[:, cols] = b3.reshape(tc, hd)
    ext_ref[0:8, :] = ext_ref[tc:tc + 8, :]
    y_ref[...] = _gelu_tanh(gate_ref[...])

    def body(g, hc):
        r0 = pl.multiple_of(g * 8, 8)
        h = a_ref[pl.ds(r0, 8), :] * hc + b_ref[pl.ds(r0, 8), :]
        y_ref[pl.ds(r0, 8), :] = h * y_ref[pl.ds(r0, 8), :]
        return jnp.broadcast_to(h[7:8, :], (8, width))

    hc = lax.fori_loop(0, ng, body, hc_ref[...], unroll=2)
    hc_ref[...] = hc

    @pl.when(t == pl.num_programs(2) - 1)
    def _():
        ht_ref[...] = hc
        tailt_ref[...] = ext_ref[0:8, :]


def _rg_lru(proj, conv_w, conv_b, wax, bax, softplus_neg_lam, h0, tail0, *, d_rg):
    bsz, t_len, _ = proj.shape
    heads, hd, _ = wax.shape
    conv_width = conv_w.shape[0]
    tc = _divisor_tile(t_len, 1024, V7X_SUBLANES)
    hpc = _divisor_tile(heads, max(1, 1024 // hd), 1)
    wd = hpc * hd
    ncell = heads // hpc
    vmem = 2 * 3 * tc * wd * 4 + (tc + 8) * wd * 4 + 2 * tc * wd * 4 + 2 * hpc * hd * 2 * hd * 2 + 8 * tc * hd * 4
    body = functools.partial(_rg_body, conv_width=conv_width)
    hsel = lambda b, h, t: (0, 0, h)
    return pl.pallas_call(
        body,
        grid=(bsz, ncell, t_len // tc),
        in_specs=[
            pl.BlockSpec((None, tc, wd), lambda b, h, t: (b, t, h)),
            pl.BlockSpec((None, tc, wd), lambda b, h, t: (b, t, ncell + h)),
            pl.BlockSpec((conv_width, wd), lambda b, h, t: (0, h)),
            pl.BlockSpec((1, wd), lambda b, h, t: (0, h)),
            pl.BlockSpec((hpc, hd, 2 * hd), lambda b, h, t: (h, 0, 0)),
            pl.BlockSpec((hpc, 1, 2 * hd), lambda b, h, t: (h, 0, 0)),
            pl.BlockSpec((1, wd), lambda b, h, t: (0, h)),
            pl.BlockSpec((None, 8, wd), hsel),
            pl.BlockSpec((None, 8, wd), hsel),
        ],
        out_specs=[
            pl.BlockSpec((None, tc, wd), lambda b, h, t: (b, t, h)),
            pl.BlockSpec((None, 8, wd), lambda b, h, t: (b, 0, h)),
            pl.BlockSpec((None, 8, wd), lambda b, h, t: (b, 0, h)),
        ],
        out_shape=[
            jax.ShapeDtypeStruct((bsz, t_len, d_rg), F32),
            jax.ShapeDtypeStruct((bsz, 8, d_rg), F32),
            jax.ShapeDtypeStruct((bsz, 8, d_rg), F32),
        ],
        scratch_shapes=[
            pltpu.VMEM((tc + 8, wd), F32),
            pltpu.VMEM((tc, wd), F32),
            pltpu.VMEM((tc, wd), F32),
            pltpu.VMEM((8, wd), F32),
        ],
        compiler_params=_params(("parallel", "parallel", "arbitrary"), vmem),
        name="rg_lru",
    )(proj, proj, conv_w, conv_b.reshape(1, d_rg), wax, bax, softplus_neg_lam.reshape(1, d_rg),
      h0, tail0)


def _s5_body(u_ref, km_ref, wsm_ref, vm_ref, e1_ref, e2_ref, mm_ref, mw_ref, mv_ref, lam_ref, p8_ref, d_ref,
             x0_ref, z_ref, xt_ref, mi_ref, ws_ref, v_ref, s_ref, xc_ref):
    t = pl.program_id(2)
    tc, slab = u_ref.shape
    nc = tc // S5_CHUNK
    ns = lam_ref.shape[1] // 2
    ng = nc // V7X_SUBLANES

    @pl.when(jnp.logical_and(pl.program_id(1) == 0, t == 0))
    def _():
        def expand(small_ref, e_ref, mask_ref, dst_ref):
            width = dst_ref.shape[1]
            piece = min(width, 2 * V7X_LANES)
            small = small_ref[...].astype(BF16)
            for q0 in range(0, width, piece):
                full = jnp.dot(small, e_ref[:, q0:q0 + piece], preferred_element_type=F32)
                dst_ref[:, q0:q0 + piece] = (full * mask_ref[:, q0:q0 + piece].astype(F32)).astype(BF16)

        expand(km_ref, e1_ref, mm_ref, mi_ref)
        expand(wsm_ref, e2_ref, mw_ref, ws_ref)
        expand(vm_ref, e1_ref, mv_ref, v_ref)

    @pl.when(t == 0)
    def _():
        xc_ref[...] = x0_ref[...]

    us = [u_ref[pl.ds(i, nc, stride=S5_CHUNK), :] for i in range(S5_CHUNK)]
    ucat = jnp.concatenate([ui.astype(BF16) for ui in us], axis=-1)
    send = jnp.dot(ucat, ws_ref[...], preferred_element_type=F32)
    ycat = jnp.dot(ucat, mi_ref[...], preferred_element_type=F32)

    xr, xi = _group_scan_const_complex(send[:, :ns].reshape(ng, 8, ns), send[:, ns:].reshape(ng, 8, ns),
                                       lam_ref, ns)
    s_ref[7:8, :] = xc_ref[0:1, :]
    s_ref[8:8 + nc, :ns] = xr.reshape(nc, ns)
    s_ref[8:8 + nc, ns:] = xi.reshape(nc, ns)

    p8r = p8_ref[:, :ns]
    p8i = p8_ref[:, ns:]

    def body(g, carry):
        cr, ci = carry
        r0 = pl.multiple_of(8 + g * 8, 8)
        nr = s_ref[pl.ds(r0, 8), :ns] + (p8r * cr - p8i * ci)
        ni = s_ref[pl.ds(r0, 8), ns:] + (p8r * ci + p8i * cr)
        s_ref[pl.ds(r0, 8), :ns] = nr
        s_ref[pl.ds(r0, 8), ns:] = ni
        return (jnp.broadcast_to(nr[7:8, :], (8, ns)), jnp.broadcast_to(ni[7:8, :], (8, ns)))

    cr, ci = lax.fori_loop(0, ng, body, (xc_ref[:, :ns], xc_ref[:, ns:]), unroll=True)
    xc_ref[:, :ns] = cr
    xc_ref[:, ns:] = ci

    x_in = s_ref[pl.ds(7, nc), :]
    ycat = ycat + jnp.dot(x_in.astype(BF16), v_ref[...], preferred_element_type=F32)
    for i in range(S5_CHUNK):
        y = ycat[:, i * slab:(i + 1) * slab] + d_ref[...] * us[i]
        z_ref[pl.ds(i, nc, stride=S5_CHUNK), :] = _gelu_tanh(y)

    @pl.when(t == pl.num_programs(2) - 1)
    def _():
        xt_ref[...] = xc_ref[...]


def _s5_scan(proj, tables, consts, d_skip, x0, *, col0, d_s5):
    bsz, t_len, _ = proj.shape
    k_m, w_m, v_m, lam_pow, p8 = tables
    e1, e2, mask_m, mask_w, mask_v = consts
    nslab, kdim, _ = k_m.shape
    slab = kdim // S5_CHUNK
    ns2 = lam_pow.shape[-1]
    tc = _divisor_tile(t_len, 4096, S5_CHUNK * V7X_SUBLANES)
    assert tc % (S5_CHUNK * V7X_SUBLANES) == 0, (t_len, tc)
    nc = tc // S5_CHUNK
    cb0 = col0 // slab
    big = 2 * kdim * kdim + kdim * ns2
    vmem = (4 * tc * slab * 4 + 2 * big * 2 + big * 2 + 2 * (2 * kdim + ns2) * slab * 4 + (nc + 8) * ns2 * 4
            + nc * (2 * kdim * 4 + kdim * 2 + 3 * ns2 * 4))
    slab_blk = lambda a: pl.BlockSpec((None,) + a.shape[1:], lambda s, b, t: (s, 0, 0))
    const_blk = lambda a: pl.BlockSpec(a.shape, lambda s, b, t: (0, 0))
    return pl.pallas_call(
        _s5_body,
        grid=(nslab, bsz, t_len // tc),
        in_specs=[
            pl.BlockSpec((None, tc, slab), lambda s, b, t: (b, t, cb0 + s)),
            slab_blk(k_m), slab_blk(w_m), slab_blk(v_m),
            const_blk(e1), const_blk(e2), const_blk(mask_m), const_blk(mask_w), const_blk(mask_v),
            pl.BlockSpec((None, 8, ns2), lambda s, b, t: (s, 0, 0)),
            pl.BlockSpec((None, 8, ns2), lambda s, b, t: (s, 0, 0)),
            pl.BlockSpec((1, slab), lambda s, b, t: (0, s)),
            pl.BlockSpec((None, None, 8, ns2), lambda s, b, t: (0, s, 0, 0)),
        ],
        out_specs=[
            pl.BlockSpec((None, tc, slab), lambda s, b, t: (b, t, s)),
            pl.BlockSpec((None, None, 8, ns2), lambda s, b, t: (b, s, 0, 0)),
        ],
        out_shape=[
            jax.ShapeDtypeStruct((bsz, t_len, d_s5), F32),
            jax.ShapeDtypeStruct((bsz, nslab, 8, ns2), F32),
        ],
        scratch_shapes=[
            pltpu.VMEM((kdim, kdim), BF16),
            pltpu.VMEM((kdim, ns2), BF16),
            pltpu.VMEM((ns2, kdim), BF16),
            pltpu.VMEM((nc + 8, ns2), F32),
            pltpu.VMEM((8, ns2), F32),
        ],
        compiler_params=_params(("parallel", "arbitrary", "arbitrary"), vmem),
        name="s5_scan",
    )(proj, k_m, w_m, v_m, e1, e2, mask_m, mask_w, mask_v, lam_pow, p8, d_skip.reshape(1, d_s5), x0)


def _glu_body(z_ref, w_ref, b_ref, o_ref, zb_ref):
    tm = z_ref.shape[0]
    tn = o_ref.shape[1]
    rc = _row_chunk(tm)

    @pl.when(pl.program_id(1) == 0)
    def _():
        def body(c, _):
            r0 = pl.multiple_of(c * rc, rc)
            zb_ref[pl.ds(r0, rc), :] = z_ref[pl.ds(r0, rc), :].astype(BF16)
            return 0

        lax.fori_loop(0, tm // rc, body, 0, unroll=2)

    pre = jnp.dot(zb_ref[...], w_ref[...], preferred_element_type=F32) + b_ref[...]
    cols = pl.ds(pl.multiple_of(pl.program_id(1) * tn, tn), tn)
    o_ref[...] = z_ref[:, cols] * _sigmoid(pre)


def _glu(z, w, b):
    m, d = z.shape
    tm = _divisor_tile(m, 1024, 16)
    tn = _divisor_tile(d, 1024, V7X_LANES)
    vmem = 2 * tm * d * 4 + tm * d * 2 + 2 * d * tn * 2 + 4 * tm * tn * 4
    return pl.pallas_call(
        _glu_body,
        grid=(m // tm, d // tn),
        in_specs=[
            pl.BlockSpec((tm, d), lambda i, j: (i, 0)),
            pl.BlockSpec((d, tn), lambda i, j: (0, j)),
            pl.BlockSpec((1, tn), lambda i, j: (0, j)),
        ],
        out_specs=pl.BlockSpec((tm, tn), lambda i, j: (i, j)),
        out_shape=jax.ShapeDtypeStruct((m, d), F32),
        scratch_shapes=[pltpu.VMEM((tm, d), BF16)],
        compiler_params=_params(("parallel", "arbitrary"), vmem),
        name="s5_glu",
    )(z, w, b.reshape(1, d))


def _outproj_body(yr_hbm, ys_hbm, gr_ref, gs_ref, w_ref, h_ref, o_ref, yn_ref, rbuf, sbuf, rsem, ssem):
    tm = yn_ref.shape[0]
    d_rg = rbuf.shape[-1]
    rc = _row_chunk(tm)

    def consume():
        def body(c, _):
            rows = pl.ds(pl.multiple_of(c * rc, rc), rc)
            yn_ref[rows, :d_rg] = _rms_rows(rbuf[rows, :], gr_ref[...]).astype(BF16)
            yn_ref[rows, d_rg:] = _rms_rows(sbuf[rows, :], gs_ref[...]).astype(BF16)
            return 0

        lax.fori_loop(0, tm // rc, body, 0, unroll=2)

    _staged_rows([yr_hbm, ys_hbm], [rbuf, sbuf], [rsem, ssem], consume)
    o_ref[...] = h_ref[...] + jnp.dot(yn_ref[...], w_ref[...], preferred_element_type=F32)


def _outproj(y_rg, y_s5, g_rg, g_s5, w, h):
    m, d_rg = y_rg.shape
    d_s5 = y_s5.shape[1]
    d_mix, d = w.shape
    tm = _divisor_tile(m, 1024, 16)
    tn = _divisor_tile(d, min(512, d // 2), V7X_LANES)
    assert d // tn >= 2, (d, tn)
    vmem = tm * d_mix * 4 + tm * d_mix * 2 + 2 * d_mix * tn * 2 + 5 * tm * tn * 4
    return pl.pallas_call(
        _outproj_body,
        grid=(m // tm, d // tn),
        in_specs=[
            pl.BlockSpec(memory_space=pl.ANY),
            pl.BlockSpec(memory_space=pl.ANY),
            pl.BlockSpec((1, d_rg), lambda i, j: (0, 0)),
            pl.BlockSpec((1, d_s5), lambda i, j: (0, 0)),
            pl.BlockSpec((d_mix, tn), lambda i, j: (0, j)),
            pl.BlockSpec((tm, tn), lambda i, j: (i, j)),
        ],
        out_specs=pl.BlockSpec((tm, tn), lambda i, j: (i, j)),
        out_shape=jax.ShapeDtypeStruct((m, d), F32),
        scratch_shapes=[
            pltpu.VMEM((tm, d_mix), BF16),
            pltpu.VMEM((tm, d_rg), F32),
            pltpu.VMEM((tm, d_s5), F32),
            pltpu.SemaphoreType.DMA((1,)),
            pltpu.SemaphoreType.DMA((1,)),
        ],
        compiler_params=_params(("arbitrary", "arbitrary"), vmem),
        name="out_proj",
    )(y_rg, y_s5, g_rg.reshape(1, d_rg), g_s5.reshape(1, d_s5), w, h)


def _s5_tables(lam_re, lam_im, log_dt, b_re, b_im, c_re, c_im, slab):
    g, n = lam_re.shape
    c = b_re.shape[-1]
    gs = slab // c
    nslab = g // gs
    ell = S5_CHUNK
    dt = jnp.exp(log_dt.astype(F32))[:, None]
    lam = lax.complex(lam_re.astype(F32), lam_im.astype(F32))
    lam_dt = lam * dt
    lam_bar = jnp.exp(lam_dt)
    b_bar = ((lam_bar - 1.0) / lam)[..., None] * lax.complex(b_re.astype(F32), b_im.astype(F32))
    cc = lax.complex(c_re.astype(F32), c_im.astype(F32))

    def lam_pow(ks):
        return jnp.exp(lam_dt[None] * jnp.asarray(ks, F32)[:, None, None])

    def state_lanes(p):
        p = p.reshape(p.shape[0], nslab, gs * n)
        return jnp.concatenate([jnp.real(p), jnp.imag(p)], axis=-1).transpose(1, 0, 2)

    lam_tab = state_lanes(lam_pow([ell, 2 * ell, 4 * ell] + [ell] * 5))
    p8 = state_lanes(lam_pow([ell * (r + 1) for r in range(8)]))

    pw = lam_pow(list(range(ell + 1)))
    exact = lax.Precision.HIGHEST

    def slab_major(a):
        return a.reshape(ell, nslab, gs * c, a.shape[-1]).transpose(1, 0, 2, 3).reshape(nslab, ell * gs * c, a.shape[-1])

    kk = jnp.real(jnp.einsum('gdn,kgn,gnc->gckd', cc, pw[:ell], b_bar)).reshape(g * c, ell * c)
    place = np.zeros((ell, ell * c, ell * c), np.float32)
    for i in range(ell):
        for k in range(ell - i):
            place[i, k * c + np.arange(c), (i + k) * c + np.arange(c)] = 1.0
    k_m = slab_major(jnp.einsum('rk,ikq->irq', kk, jnp.asarray(place), precision=exact))
    p_st = pw[ell - 1 - np.arange(ell)][:, :, None, :]
    b_t = b_bar.transpose(0, 2, 1)[None]
    pr, pi, br, bi = jnp.real(p_st), jnp.imag(p_st), jnp.real(b_t), jnp.imag(b_t)
    w_m = slab_major(jnp.concatenate([pr * br - pi * bi, pr * bi + pi * br], axis=-1).reshape(ell, g * c, 2 * n))
    rep_i = np.kron(np.eye(ell, dtype=np.float32), np.ones((1, c), np.float32))
    rep_c = np.kron(np.ones((1, ell), np.float32), np.eye(c, dtype=np.float32))
    p_out = pw[1:ell + 1].transpose(1, 2, 0).reshape(g * n, ell)
    c_t = cc.transpose(0, 2, 1).reshape(g * n, c)
    ar, ai = (jnp.dot(part(p_out), rep_i, precision=exact) for part in (jnp.real, jnp.imag))
    cr, ci = (jnp.dot(part(c_t), rep_c, precision=exact) for part in (jnp.real, jnp.imag))
    v_m = jnp.concatenate([(cr * ar - ci * ai).reshape(nslab, gs * n, ell * c),
                           -(cr * ai + ci * ar).reshape(nslab, gs * n, ell * c)], axis=1)
    return k_m, w_m, v_m, lam_tab, p8


def _s5_spread_consts(gs, c, n):
    ell = S5_CHUNK
    a1, q1 = np.arange(ell * c)[:, None], np.arange(ell * gs * c)[None, :]
    e1 = (a1 // c == q1 // (gs * c)) & (a1 % c == q1 % c)
    a2, q2 = np.arange(2 * n)[:, None], np.arange(2 * gs * n)[None, :]
    e2 = (a2 // n == q2 // (gs * n)) & (a2 % n == q2 % n)
    g_in = (np.arange(ell * gs * c) // c) % gs
    g_st = (np.arange(2 * gs * n) % (gs * n)) // n
    mask_m = g_in[:, None] == g_in[None, :]
    mask_w = g_in[:, None] == g_st[None, :]
    mask_v = g_st[:, None] == g_in[None, :]
    return tuple(jnp.asarray(m, dtype=BF16) for m in (e1, e2, mask_m, mask_w, mask_v))


def kernel(x, meta_tokens, ffn1_norm, ffn1_w_gate, ffn1_w_up, ffn1_w_down, mix_norm, w_in,
           rg_conv_w, rg_conv_b, rg_w_a, rg_b_a, rg_w_x, rg_b_x, rg_lambda,
           s5_lambda_re, s5_lambda_im, s5_log_dt, s5_b_re, s5_b_im, s5_c_re, s5_c_im, s5_d,
           s5_glu_w, s5_glu_b, rg_out_norm, s5_out_norm, w_out,
           ffn2_norm, ffn2_w_gate, ffn2_w_up, ffn2_w_down, final_norm):
    bsz, t_len, d = x.shape
    depth = ffn1_norm.shape[0]
    d_rg = rg_lambda.shape[-1]
    d_s5 = s5_d.shape[-1]
    heads, hd = rg_w_a.shape[1], rg_w_a.shape[2]
    slab = V7X_LANES
    s5_align = S5_CHUNK * V7X_SUBLANES

    h = x.reshape(bsz * t_len, d)
    hm = meta_tokens.astype(x.dtype)
    n_meta = hm.shape[0]
    for l in range(depth):
        last = l == depth - 1
        ffn1_w = (ffn1_w_gate[l].astype(BF16), ffn1_w_up[l].astype(BF16), ffn1_w_down[l].astype(BF16))
        ffn2_w = (ffn2_w_gate[l].astype(BF16), ffn2_w_up[l].astype(BF16), ffn2_w_down[l].astype(BF16))
        w_in_b = w_in[l].astype(BF16)
        w_out_b = w_out[l].astype(BF16)
        glu_w_b = s5_glu_w[l].astype(BF16)
        wax = jnp.concatenate([rg_w_a[l], rg_w_x[l]], axis=-1).astype(BF16)
        bax = jnp.concatenate([rg_b_a[l], rg_b_x[l]], axis=-1).reshape(heads, 1, 2 * hd)
        sp = jax.nn.softplus(-rg_lambda[l].astype(F32))
        s5_mats = _s5_tables(s5_lambda_re[l], s5_lambda_im[l], s5_log_dt[l], s5_b_re[l], s5_b_im[l],
                             s5_c_re[l], s5_c_im[l], slab)
        nslab, _, ns2 = s5_mats[3].shape
        s5_consts = _s5_spread_consts(slab // s5_b_re.shape[-1], s5_b_re.shape[-1], s5_lambda_re.shape[-1])
        fnorm = final_norm if last else ffn2_norm[l]

        def mixers(hrows, nb, h0, tail0, x0, front_pad):
            h1 = _ffn(hrows, ffn1_norm[l], *ffn1_w, ffn1_norm[l], final_norm=False)
            proj = _inproj(h1, mix_norm[l], w_in_b).reshape(nb, hrows.shape[0] // nb, -1)
            y_rg, h_t, tail_t = _rg_lru(proj, rg_conv_w[l], rg_conv_b[l], wax, bax, sp, h0, tail0, d_rg=d_rg)
            proj_s5 = jnp.pad(proj, ((0, 0), (front_pad, 0), (0, 0))) if front_pad else proj
            z, x_t = _s5_scan(proj_s5, s5_mats, s5_consts, s5_d[l], x0, col0=2 * d_rg, d_s5=d_s5)
            return h1, y_rg, z[:, front_pad:], h_t, tail_t, x_t

        def tail_layers(h1, y_rg, z):
            m = h1.shape[0]
            y_s5 = _glu(z.reshape(m, d_s5), glu_w_b, s5_glu_b[l])
            h2 = _outproj(y_rg.reshape(m, d_rg), y_s5, rg_out_norm[l], s5_out_norm[l], w_out_b, h1)
            return _ffn(h2, ffn2_norm[l], *ffn2_w, fnorm, final_norm=last)

        assert t_len % s5_align == 0, t_len
        zeros_rg = jnp.zeros((1, 8, d_rg), F32)
        zeros_s5 = jnp.zeros((1, nslab, 8, ns2), F32)
        meta_pad = (-n_meta) % s5_align
        h1m, y_rgm, zm, h_t, tail_t, x_t = mixers(hm, 1, zeros_rg, zeros_rg, zeros_s5, meta_pad)
        h1, y_rg, z, _, _, _ = mixers(h, bsz, h_t, tail_t, x_t, 0)
        h = tail_layers(h1, y_rg, z)
        if not last:
            hm = tail_layers(h1m, y_rgm, zm)
    return h.reshape(bsz, t_len, d)
```
